```python
import jax, jax.numpy as jnp
from jax import lax
import numpy as np

D_MODEL = 2048
BATCH = 16
SEQ = 2048
DEPTH = 2

CHUNK = 64
Q_BLOCK = 128
EPS = 1e-6
PLE_DIM = 256

A_HEADS = 8
A_LATENT = 128
A_TOPK_MAX = 256
IDX_HEADS = 16
IDX_DIM = 64
B_HEADS = 4
B_DK = 128
B_DV = 256
B_GATE_RANK = 16
B_GATE_NORM = 16.0
C_HEADS = 16
C_HEAD_DIM = D_MODEL // C_HEADS
C_WIDTH = C_HEADS * C_HEAD_DIM
FFN_HIDDEN = -(-8 * D_MODEL // (3 * 256)) * 256

EVEN_WIDTHS = (A_HEADS * A_LATENT, A_LATENT, IDX_HEADS * IDX_DIM, IDX_DIM, IDX_HEADS,
               B_HEADS * B_DK, B_HEADS * B_DK, B_HEADS * B_DV, B_HEADS * B_DV, B_GATE_RANK)
EVEN_IN = sum(EVEN_WIDTHS)
EVEN_SPLITS = [sum(EVEN_WIDTHS[:i + 1]) for i in range(len(EVEN_WIDTHS) - 1)]
EVEN_OUT = A_HEADS * A_LATENT + B_HEADS * B_DV

kernel_name = 'hybrid_dsa_gla_stickbreaking'


def _rmsnorm(x, g):
    xf = x.astype(jnp.float32)
    y = xf * lax.rsqrt(jnp.mean(xf * xf, axis=-1, keepdims=True) + EPS)
    return (y * g.astype(jnp.float32)).astype(x.dtype)


def _dsa(q_a, c, q_i, k_i, w_i):
    b, s_len = q_a.shape[0], q_a.shape[1]
    topk = min(A_TOPK_MAX, s_len // 4)
    nb = s_len // Q_BLOCK
    key_pos = jnp.arange(s_len, dtype=jnp.int32)
    starts = jnp.arange(nb, dtype=jnp.int32) * Q_BLOCK

    def to_blocks(t):
        return jnp.moveaxis(t.reshape((b, nb, Q_BLOCK) + t.shape[2:]), 1, 0)

    def one_block(args):
        qa, qi, wi, start = args
        limit = ((start + jnp.arange(Q_BLOCK, dtype=jnp.int32)) // CHUNK + 1) * CHUNK
        admissible = key_pos[None, :] < limit[:, None]
        logits = jnp.einsum('bqhd,bsd->bqsh', qi, k_i).astype(jnp.float32) * IDX_DIM ** -0.5
        score = jnp.einsum('bqsh,bqh->bqs', jax.nn.relu(logits),
                           wi.astype(jnp.float32)) * IDX_HEADS ** -0.5
        score = jnp.where(admissible[None], score, -jnp.inf)
        _, idx = lax.top_k(score, topk)
        c_sel = jax.vmap(lambda cb, ib: cb[ib])(c, idx)
        valid = idx < limit[None, :, None]
        att = jnp.einsum('bqhd,bqkd->bqhk', qa, c_sel).astype(jnp.float32) * A_LATENT ** -0.5
        att = jnp.where(valid[:, :, None, :], att, -jnp.inf)
        prob = jax.nn.softmax(att, axis=-1).astype(c.dtype)
        return jnp.einsum('bqhk,bqkd->bqhd', prob, c_sel)

    out = lax.map(one_block, (to_blocks(q_a), to_blocks(q_i), to_blocks(w_i), starts))
    return jnp.moveaxis(out, 0, 1).reshape(b, s_len, A_HEADS * A_LATENT)


def _gla(q, k, v, gk):
    b, s_len, h, dk = q.shape
    dv = v.shape[-1]
    nc = s_len // CHUNK

    def to_chunks(t):
        return t.reshape(b, nc, CHUNK, h, t.shape[-1]).transpose(0, 3, 1, 2, 4)

    f32 = jnp.float32
    q = to_chunks(q).astype(f32) * dk ** -0.5
    k = to_chunks(k).astype(f32)
    v = to_chunks(v).astype(f32)
    G = jnp.cumsum(to_chunks(gk.astype(f32)), axis=3)
    G_last = G[:, :, :, -1:, :]
    q_e = q * jnp.exp(G)
    k_e = k * jnp.exp(-G)
    k_d = k * jnp.exp(G_last - G)
    causal = jnp.tril(jnp.ones((CHUNK, CHUNK), dtype=bool))
    att = jnp.where(causal, jnp.einsum('bhncd,bhnsd->bhncs', q_e, k_e), 0.0)
    o = jnp.einsum('bhncs,bhnsv->bhncv', att, v)
    kv = jnp.einsum('bhnsd,bhnsv->bhndv', k_d, v)
    decay = jnp.exp(G_last[:, :, :, 0, :])

    def step(state, inp):
        dec, inc = inp
        return dec[..., None] * state + inc, state

    s0 = jnp.zeros((b, h, dk, dv), f32)
    _, s_prev = lax.scan(step, s0, (jnp.moveaxis(decay, 2, 0), jnp.moveaxis(kv, 2, 0)))
    s_prev = jnp.moveaxis(s_prev, 0, 2)
    o = o + jnp.einsum('bhncd,bhndv->bhncv', q_e, s_prev)
    return o.transpose(0, 2, 3, 1, 4).reshape(b, s_len, h, dv)


def _stick_breaking(q, k, v):
    b, s_len, h, d = q.shape
    q, k, v = (t.transpose(0, 2, 1, 3) for t in (q, k, v))
    outs = []
    for n in range(s_len // Q_BLOCK):
        start, end = n * Q_BLOCK, (n + 1) * Q_BLOCK
        z = jnp.einsum('bhqd,bhsd->bhqs', q[:, :, start:end], k[:, :, :end]).astype(jnp.float32) * d ** -0.5
        qpos = start + jnp.arange(Q_BLOCK)
        kpos = jnp.arange(end)
        mask = kpos[None, :] < qpos[:, None]
        log_1mb = jnp.where(mask, -jax.nn.softplus(z), 0.0)
        suffix = lax.cumsum(log_1mb, axis=3, reverse=True) - log_1mb
        weight = jnp.where(mask, jnp.exp(suffix - jax.nn.softplus(-z)), 0.0)
        outs.append(jnp.einsum('bhqs,bhsd->bhqd', weight.astype(v.dtype), v[:, :, :end]))
    o = jnp.concatenate(outs, axis=2)
    return o.transpose(0, 2, 1, 3).reshape(b, s_len, h * d)


def _even_mixer(h, norm_g, w_in, w_gk, b_gk, gla_norm, w_out):
    b, s_len, _ = h.shape
    u = _rmsnorm(h, norm_g) @ w_in
    q_a, c_a, q_i, k_i, w_i, q_b, k_b, v_b, g_b, gk_lr = jnp.split(u, EVEN_SPLITS, axis=-1)
    o_a = _dsa(q_a.reshape(b, s_len, A_HEADS, A_LATENT), c_a,
               q_i.reshape(b, s_len, IDX_HEADS, IDX_DIM), k_i, w_i)
    gk = jax.nn.log_sigmoid((gk_lr @ w_gk + b_gk).astype(jnp.float32)) / B_GATE_NORM
    o_b = _gla(q_b.reshape(b, s_len, B_HEADS, B_DK), k_b.reshape(b, s_len, B_HEADS, B_DK),
               v_b.reshape(b, s_len, B_HEADS, B_DV), gk.reshape(b, s_len, B_HEADS, B_DK))
    o_b = _rmsnorm(o_b, gla_norm) * jax.nn.silu(g_b.reshape(b, s_len, B_HEADS, B_DV).astype(jnp.float32))
    o = jnp.concatenate([o_a.astype(h.dtype), o_b.reshape(b, s_len, B_HEADS * B_DV).astype(h.dtype)], axis=-1)
    return o @ w_out


def _odd_mixer(h, norm_g, w_in, w_out):
    b, s_len, _ = h.shape
    q, k, v = jnp.split(_rmsnorm(h, norm_g) @ w_in, 3, axis=-1)
    shp = (b, s_len, C_HEADS, C_HEAD_DIM)
    o = _stick_breaking(q.reshape(shp), k.reshape(shp), v.reshape(shp))
    return o.astype(h.dtype) @ w_out


def _swiglu(h, norm_g, w_gate, w_up, w_down):
    xn = _rmsnorm(h, norm_g)
    return (jax.nn.silu(xn @ w_gate) * (xn @ w_up)) @ w_down


def setup_inputs(seed: int = 0) -> dict:
    key = jax.random.key(seed)
    ks = jax.random.split(key, 20)
    f32 = jnp.float32
    ne = (DEPTH + 1) // 2
    no = DEPTH // 2

    def dense(k, shape):
        return jax.random.normal(k, shape, f32) * shape[-2] ** -0.5

    def gain(k, shape):
        return 1.0 + 0.02 * jax.random.normal(k, shape, f32)

    return {
        'x': jax.random.normal(ks[0], (BATCH, SEQ, D_MODEL), f32),
        'p': jax.random.normal(ks[1], (DEPTH, BATCH, SEQ, PLE_DIM), f32),
        'even_norm': gain(ks[2], (ne, D_MODEL)),
        'even_w_in': dense(ks[3], (ne, D_MODEL, EVEN_IN)),
        'even_w_gk': dense(ks[4], (ne, B_GATE_RANK, B_HEADS * B_DK)),
        'even_b_gk': 0.01 * jax.random.normal(ks[5], (ne, B_HEADS * B_DK), f32),
        'even_gla_norm': gain(ks[6], (ne, B_DV)),
        'even_w_out': dense(ks[7], (ne, EVEN_OUT, D_MODEL)),
        'odd_norm': gain(ks[8], (no, D_MODEL)),
        'odd_w_in': dense(ks[9], (no, D_MODEL, 3 * C_WIDTH)),
        'odd_w_out': dense(ks[10], (no, C_WIDTH, D_MODEL)),
        'ffn_norm': gain(ks[11], (DEPTH, D_MODEL)),
        'ffn_w_gate': dense(ks[12], (DEPTH, D_MODEL, FFN_HIDDEN)),
        'ffn_w_up': dense(ks[13], (DEPTH, D_MODEL, FFN_HIDDEN)),
        'ffn_w_down': dense(ks[14], (DEPTH, FFN_HIDDEN, D_MODEL)),
        'ple_norm': gain(ks[15], (DEPTH, D_MODEL)),
        'ple_w_gate': dense(ks[16], (DEPTH, D_MODEL, D_MODEL)),
        'ple_w_proj': dense(ks[17], (DEPTH, PLE_DIM, D_MODEL)),
        'final_norm': gain(ks[18], (D_MODEL,)),
    }


def reference(x, p, even_norm, even_w_in, even_w_gk, even_b_gk, even_gla_norm, even_w_out,
              odd_norm, odd_w_in, odd_w_out, ffn_norm, ffn_w_gate, ffn_w_up, ffn_w_down,
              ple_norm, ple_w_gate, ple_w_proj, final_norm):
    h = x
    for i in range(DEPTH):
        j = i // 2
        if i % 2 == 0:
            h = h + _even_mixer(h, even_norm[j], even_w_in[j], even_w_gk[j], even_b_gk[j],
                                even_gla_norm[j], even_w_out[j])
        else:
            h = h + _odd_mixer(h, odd_norm[j], odd_w_in[j], odd_w_out[j])
        h = h + _swiglu(h, ffn_norm[i], ffn_w_gate[i], ffn_w_up[i], ffn_w_down[i])
        gate = jax.nn.sigmoid(_rmsnorm(h, ple_norm[i]) @ ple_w_gate[i])
        h = h + (p[i] @ ple_w_proj[i]) * gate
    return _rmsnorm(h, final_norm)
```

```python
import functools

import jax
import jax.numpy as jnp
from jax import lax
from jax.experimental import pallas as pl
from jax.experimental.pallas import tpu as pltpu

F32 = jnp.float32
BF16 = jnp.bfloat16
I32 = jnp.int32

EPS = 1e-6
CHUNK = 64
CHUNK_SHIFT = CHUNK.bit_length() - 1
A_HEADS = 8
A_LATENT = 128
A_TOPK_MAX = 256
IDX_HEADS = 16
IDX_DIM = 64
B_HEADS = 4
B_DK = 128
B_DV = 256
B_GATE_RANK = 16
B_GATE_NORM = 16.0
C_HEADS = 16

LANES = 128
SUBLANES = 8
VMEM_LIMIT_BYTES = 56 * 1024 * 1024

U_QA = 0
U_QI = 1024
U_VB = 2048
U_GB = 3072
U_QB = 4096
U_KB = 4608
U_CA = 5120
U_SMALL = 5248
U_WIDTH = 5376
SM_W = 64
SM_GK = 80

NT_DIMS = (((1,), (1,)), ((), ()))
TN_DIMS = (((0,), (0,)), ((), ()))

KEY_NEG_INF = -2139095041
INT_MIN = -2147483648


def _params(*sem):
    return pltpu.CompilerParams(dimension_semantics=sem, vmem_limit_bytes=VMEM_LIMIT_BYTES)


def _tile(n, pref):
    if n <= pref:
        return n
    t = (pref // LANES) * LANES
    while n % t:
        t -= LANES
    return t


def _rms(x, g):
    ms = jnp.mean(x * x, axis=-1, keepdims=True)
    return x * lax.rsqrt(ms + EPS) * g


def _dot(a, b):
    return jnp.dot(a, b, preferred_element_type=F32)


def _dot_nt(a, b):
    return lax.dot_general(a, b, NT_DIMS, preferred_element_type=F32)


def _softplus_tail(z):
    return jnp.log1p(jnp.exp(-jnp.abs(z)))


def _split_bf16(x):
    hi = x.astype(BF16)
    lo = (x - hi.astype(F32)).astype(BF16)
    return hi, lo


def _norm_matmul_kernel(x_ref, g_ref, w_ref, o_ref, xn_ref):
    @pl.when(pl.program_id(1) == 0)
    def _():
        xn_ref[...] = _rms(x_ref[...], g_ref[...]).astype(BF16)

    o_ref[...] = _dot(xn_ref[...], w_ref[...]).astype(o_ref.dtype)


def _norm_matmul(x, g, w, tm, tn):
    m, k = x.shape
    n = w.shape[1]
    return pl.pallas_call(
        _norm_matmul_kernel,
        grid=(m // tm, n // tn),
        in_specs=[
            pl.BlockSpec((tm, k), lambda i, j: (i, 0)),
            pl.BlockSpec((1, k), lambda i, j: (0, 0)),
            pl.BlockSpec((k, tn), lambda i, j: (0, j)),
        ],
        out_specs=pl.BlockSpec((tm, tn), lambda i, j: (i, j)),
        out_shape=jax.ShapeDtypeStruct((m, n), BF16),
        scratch_shapes=[pltpu.VMEM((tm, k), BF16)],
        compiler_params=_params("parallel", "arbitrary"),
        name="norm_matmul",
    )(x, g.reshape(1, k), w)


def _proj_residual_kernel(*refs, n_in):
    h_ref = refs[0]
    a_refs = refs[1:1 + n_in]
    w_refs = refs[1 + n_in:1 + 2 * n_in]
    o_ref = refs[1 + 2 * n_in]
    acc = h_ref[...]
    for a_ref, w_ref in zip(a_refs, w_refs):
        acc = acc + _dot(a_ref[...], w_ref[...])
    o_ref[...] = acc


def _proj_residual(h, a_list, w_list, tm, tn):
    m, n = h.shape
    n_in = len(a_list)
    in_specs = [pl.BlockSpec((tm, tn), lambda i, j: (i, j))]
    in_specs += [pl.BlockSpec((tm, a.shape[1]), lambda i, j: (i, 0)) for a in a_list]
    in_specs += [pl.BlockSpec((w.shape[0], tn), lambda i, j: (0, j)) for w in w_list]
    return pl.pallas_call(
        functools.partial(_proj_residual_kernel, n_in=n_in),
        grid=(m // tm, n // tn),
        in_specs=in_specs,
        out_specs=pl.BlockSpec((tm, tn), lambda i, j: (i, j)),
        out_shape=jax.ShapeDtypeStruct((m, n), F32),
        compiler_params=_params("parallel", "arbitrary"),
        name="proj_residual",
    )(h, *a_list, *w_list)


def _ffn_kernel(x_ref, g_ref, wg_ref, wu_ref, wd_ref, o_ref, xn_ref):
    @pl.when(pl.program_id(1) == 0)
    def _():
        x = x_ref[...]
        xn_ref[...] = _rms(x, g_ref[...]).astype(BF16)
        o_ref[...] = x

    xn = xn_ref[...]
    gate = _dot(xn, wg_ref[...])
    up = _dot(xn, wu_ref[...])
    act = (gate * jax.nn.sigmoid(gate) * up).astype(BF16)
    o_ref[...] += _dot(act, wd_ref[...])


def _ffn(h, g, wg, wu, wd, tm, th):
    m, d = h.shape
    hid = wg.shape[1]
    return pl.pallas_call(
        _ffn_kernel,
        grid=(m // tm, hid // th),
        in_specs=[
            pl.BlockSpec((tm, d), lambda i, j: (i, 0)),
            pl.BlockSpec((1, d), lambda i, j: (0, 0)),
            pl.BlockSpec((d, th), lambda i, j: (0, j)),
            pl.BlockSpec((d, th), lambda i, j: (0, j)),
            pl.BlockSpec((th, d), lambda i, j: (j, 0)),
        ],
        out_specs=pl.BlockSpec((tm, d), lambda i, j: (i, 0)),
        out_shape=jax.ShapeDtypeStruct((m, d), F32),
        scratch_shapes=[pltpu.VMEM((tm, d), BF16)],
        compiler_params=_params("parallel", "arbitrary"),
        name="ffn",
    )(h, g.reshape(1, d), wg, wu, wd)


def _ple_kernel(x_ref, g_ref, wg_ref, p_ref, wp_ref, fg_ref, o_ref, xn_ref, *, tn, final_norm):
    j = pl.program_id(1)

    @pl.when(j == 0)
    def _():
        xn_ref[...] = _rms(x_ref[...], g_ref[...]).astype(BF16)

    cols = pl.ds(pl.multiple_of(j * tn, tn), tn)
    gate = jax.nn.sigmoid(_dot(xn_ref[...], wg_ref[...]))
    pe = _dot(p_ref[...].astype(BF16), wp_ref[...])
    o_ref[:, cols] = x_ref[:, cols] + pe * gate

    if final_norm:
        @pl.when(j == pl.num_programs(1) - 1)
        def _():
            o_ref[...] = _rms(o_ref[...], fg_ref[...])


def _ple(h, g, wg, p, wp, fg, tm, tn, final_norm):
    m, d = h.shape
    pd = p.shape[1]
    return pl.pallas_call(
        functools.partial(_ple_kernel, tn=tn, final_norm=final_norm),
        grid=(m // tm, d // tn),
        in_specs=[
            pl.BlockSpec((tm, d), lambda i, j: (i, 0)),
            pl.BlockSpec((1, d), lambda i, j: (0, 0)),
            pl.BlockSpec((d, tn), lambda i, j: (0, j)),
            pl.BlockSpec((tm, pd), lambda i, j: (i, 0)),
            pl.BlockSpec((pd, tn), lambda i, j: (0, j)),
            pl.BlockSpec((1, d), lambda i, j: (0, 0)),
        ],
        out_specs=pl.BlockSpec((tm, d), lambda i, j: (i, 0)),
        out_shape=jax.ShapeDtypeStruct((m, d), F32),
        scratch_shapes=[pltpu.VMEM((tm, d), BF16)],
        compiler_params=_params("parallel", "arbitrary"),
        name="ple",
    )(h, g.reshape(1, d), wg, p, wp, fg.reshape(1, d))


def _key_to_float(key):
    bits = key ^ ((key >> 31) & 0x7FFFFFFF)
    return lax.bitcast_convert_type(bits, F32)


def _dsa_kernel(qi_ref, qsm_ref, ksm_ref, qa_ref, c_ref, o_ref,
                sc_ref, ct_ref, acc_ref, m_ref, l_ref, *, topk, qb_size, seq):
    t = qb_size
    qb = pl.program_id(1)
    nk = qb + 1

    @pl.when(qb == 0)
    def _():
        ct_ref[...] = jnp.transpose(c_ref[...].astype(F32)).astype(BF16)

    row = lax.broadcasted_iota(I32, (t, t), 0)
    col = lax.broadcasted_iota(I32, (t, t), 1)
    limit = (((qb * t + col) >> CHUNK_SHIFT) + 1) << CHUNK_SHIFT

    w_t = jnp.transpose(qsm_ref[...].astype(F32))[SM_W:SM_W + IDX_HEADS, :]

    def score_tile(kt, carry):
        rows = pl.ds(pl.multiple_of(kt * t, t), t)
        kk = ksm_ref[rows, :][:, :IDX_DIM]
        sc = jnp.zeros((t, t), F32)
        for h in range(IDX_HEADS):
            qh = qi_ref[:, h * IDX_DIM:(h + 1) * IDX_DIM]
            lg = _dot_nt(kk, qh) * (IDX_DIM ** -0.5)
            sc = sc + jnp.maximum(lg, 0.0) * w_t[h:h + 1, :]
        sc = sc * (IDX_HEADS ** -0.5)
        sc_ref[rows, :] = jnp.where(kt * t + row < limit, sc, -jnp.inf)
        return carry

    lax.fori_loop(0, nk, score_tile, 0)

    def count(pred):
        def body(kt, acc):
            rows = pl.ds(pl.multiple_of(kt * t, t), t)
            hit = pred(sc_ref[rows, :]).astype(I32)
            return acc + jnp.sum(hit.reshape(t // SUBLANES, SUBLANES, t), axis=0)
        acc = lax.fori_loop(0, nk, body, jnp.zeros((SUBLANES, t), I32))
        return jnp.sum(acc, axis=0, keepdims=True)

    def count_ge(key):
        cand = _key_to_float(key)
        return count(lambda s: s >= cand)

    zero_key = jnp.zeros((1, t), I32)
    res0 = jnp.where(count_ge(zero_key) >= topk, 0, INT_MIN).astype(I32)

    def bisect(i, res):
        cand = res | jnp.left_shift(jnp.int32(1), 30 - i)
        return jnp.where(count_ge(cand) >= topk, cand, res)

    res = lax.fori_loop(0, 31, bisect, res0)
    thr = _key_to_float(jnp.maximum(res, KEY_NEG_INF))
    n_ge = count(lambda s: s >= thr) + jnp.where(thr == -jnp.inf, seq - nk * t, 0)

    @pl.when(jnp.max(n_ge) > topk)
    def _():
        need = (topk - count(lambda s: s > thr)).astype(F32)
        tri = (row >= col).astype(BF16)

        def body(kt, seen):
            rows = pl.ds(pl.multiple_of(kt * t, t), t)
            s = sc_ref[rows, :]
            eq = s == thr
            rank = _dot(tri, eq.astype(BF16)) + seen
            drop = (eq & (rank > need)) | (kt * t + row >= limit)
            sc_ref[rows, :] = jnp.where(drop, jnp.nan, s)
            return seen + jnp.sum(eq.astype(F32), axis=0, keepdims=True)

        lax.fori_loop(0, nk, body, jnp.zeros((1, t), F32))

    m_ref[...] = jnp.full(m_ref.shape, -1e30, F32)
    l_ref[...] = jnp.zeros(l_ref.shape, F32)
    acc_ref[...] = jnp.zeros(acc_ref.shape, F32)

    def attend(kt, carry):
        rows = pl.ds(pl.multiple_of(kt * t, t), t)
        valid = sc_ref[rows, :] >= thr
        ck = c_ref[rows, :]
        ckt = ct_ref[:, rows]
        for h in range(A_HEADS):
            qh = qa_ref[:, h * A_LATENT:(h + 1) * A_LATENT]
            s = jnp.where(valid, _dot_nt(ck, qh) * (A_LATENT ** -0.5), -1e30)
            m_old = m_ref[h:h + 1, :]
            m_new = jnp.maximum(m_old, jnp.max(s, axis=0, keepdims=True))
            alpha = jnp.exp(m_old - m_new)
            p = jnp.exp(s - m_new)
            l_ref[h:h + 1, :] = alpha * l_ref[h:h + 1, :] + jnp.sum(p, axis=0, keepdims=True)
            hs = slice(h * A_LATENT, (h + 1) * A_LATENT)
            acc_ref[hs, :] = alpha * acc_ref[hs, :] + _dot(ckt, p.astype(BF16))
            m_ref[h:h + 1, :] = m_new
        return carry

    lax.fori_loop(0, nk, attend, 0)

    for h in range(A_HEADS):
        hs = slice(h * A_LATENT, (h + 1) * A_LATENT)
        out = acc_ref[hs, :] / l_ref[h:h + 1, :]
        o_ref[:, hs] = jnp.transpose(out).astype(o_ref.dtype)


def _dsa(u, batch, seq, qb_size):
    m = batch * seq
    nqb = seq // qb_size
    topk = min(A_TOPK_MAX, seq // 4)
    wide = A_HEADS * A_LATENT
    return pl.pallas_call(
        functools.partial(_dsa_kernel, topk=topk, qb_size=qb_size, seq=seq),
        grid=(batch, nqb),
        in_specs=[
            pl.BlockSpec((qb_size, wide), lambda b, q: (b * nqb + q, U_QI // wide)),
            pl.BlockSpec((qb_size, LANES), lambda b, q: (b * nqb + q, U_SMALL // LANES)),
            pl.BlockSpec((seq, LANES), lambda b, q: (b, U_SMALL // LANES)),
            pl.BlockSpec((qb_size, wide), lambda b, q: (b * nqb + q, U_QA // wide)),
            pl.BlockSpec((seq, LANES), lambda b, q: (b, U_CA // LANES)),
        ],
        out_specs=pl.BlockSpec((qb_size, wide), lambda b, q: (b * nqb + q, 0)),
        out_shape=jax.ShapeDtypeStruct((m, wide), BF16),
        scratch_shapes=[
            pltpu.VMEM((seq, qb_size), F32),
            pltpu.VMEM((A_LATENT, seq), BF16),
            pltpu.VMEM((wide, qb_size), F32),
            pltpu.VMEM((A_HEADS, qb_size), F32),
            pltpu.VMEM((A_HEADS, qb_size), F32),
        ],
        compiler_params=_params("parallel", "arbitrary"),
        name="dsa",
    )(u, u, u, u, u)


def _gla_kernel(q_ref, k_ref, v_ref, g_ref, sm_ref, wgk_ref, bgk_ref, gn_ref, o_ref, st_ref,
                *, nchunks):
    c = CHUNK
    row = lax.broadcasted_iota(I32, (c, c), 0)
    col = lax.broadcasted_iota(I32, (c, c), 1)
    causal = row >= col
    tri = causal.astype(BF16)
    st_ref[...] = jnp.zeros(st_ref.shape, F32)
    wgk = wgk_ref[...]
    bgk = bgk_ref[...]
    gn = gn_ref[...]

    def chunk(ci, carry):
        rows = pl.ds(pl.multiple_of(ci * c, c), c)
        x = _dot(sm_ref[rows, :], wgk) + bgk
        gk = -(jnp.maximum(-x, 0.0) + _softplus_tail(x)) / B_GATE_NORM
        hi, lo = _split_bf16(gk)
        cum = _dot(tri, hi) + _dot(tri, lo)
        last = cum[c - 1:c, :]
        q = q_ref[rows, :].astype(F32) * (B_DK ** -0.5)
        k = k_ref[rows, :].astype(F32)
        v = v_ref[rows, :]
        q_e = (q * jnp.exp(cum)).astype(BF16)
        k_e = (k * jnp.exp(-cum)).astype(BF16)
        k_d = (k * jnp.exp(last - cum)).astype(BF16)
        att = jnp.where(causal, _dot_nt(q_e, k_e), 0.0).astype(BF16)
        st = st_ref[...]
        o = _dot(att, v) + _dot_nt(q_e, st.astype(BF16))
        st_ref[...] = st * jnp.exp(last) + lax.dot_general(v, k_d, TN_DIMS, preferred_element_type=F32)
        gate = g_ref[rows, :].astype(F32)
        o_ref[rows, :] = (_rms(o, gn) * (gate * jax.nn.sigmoid(gate))).astype(o_ref.dtype)
        return carry

    lax.fori_loop(0, nchunks, chunk, 0)


def _gla(u, wgk_pad, bgk, gnorm, batch, seq):
    m = batch * seq
    return pl.pallas_call(
        functools.partial(_gla_kernel, nchunks=seq // CHUNK),
        grid=(batch, B_HEADS),
        in_specs=[
            pl.BlockSpec((seq, B_DK), lambda b, h: (b, U_QB // B_DK + h)),
            pl.BlockSpec((seq, B_DK), lambda b, h: (b, U_KB // B_DK + h)),
            pl.BlockSpec((seq, B_DV), lambda b, h: (b, U_VB // B_DV + h)),
            pl.BlockSpec((seq, B_DV), lambda b, h: (b, U_GB // B_DV + h)),
            pl.BlockSpec((seq, LANES), lambda b, h: (b, U_SMALL // LANES)),
            pl.BlockSpec((LANES, B_DK), lambda b, h: (0, h)),
            pl.BlockSpec((1, B_DK), lambda b, h: (0, h)),
            pl.BlockSpec((1, B_DV), lambda b, h: (0, 0)),
        ],
        out_specs=pl.BlockSpec((seq, B_DV), lambda b, h: (b, h)),
        out_shape=jax.ShapeDtypeStruct((m, B_HEADS * B_DV), BF16),
        scratch_shapes=[pltpu.VMEM((B_DV, B_DK), F32)],
        compiler_params=_params("parallel", "arbitrary"),
        name="gla",
    )(u, u, u, u, u, wgk_pad, bgk.reshape(1, -1), gnorm.reshape(1, -1))


def _sb_kernel(q_ref, k_ref, v_ref, o_ref, *, blk, nblk, head_dim):
    t = blk
    row = lax.broadcasted_iota(I32, (t, t), 0)
    col = lax.broadcasted_iota(I32, (t, t), 1)
    upper = (row > col).astype(BF16)
    scale = head_dim ** -0.5

    def q_block(qb, carry):
        qrows = pl.ds(pl.multiple_of(qb * t, t), t)
        q = q_ref[qrows, :]

        def k_block(i, state):
            run, acc = state
            krows = pl.ds(pl.multiple_of((qb - i) * t, t), t)
            z = _dot_nt(q, k_ref[krows, :]) * scale
            tail = _softplus_tail(z)
            log_1mb = -(jnp.maximum(z, 0.0) + tail)
            log_b = jnp.minimum(z, 0.0) - tail
            mask = col < row + i * t
            lm = jnp.where(mask, log_1mb, 0.0)
            hi, lo = _split_bf16(lm)
            suffix = _dot(hi, upper) + _dot(lo, upper) + run
            w = jnp.where(mask, jnp.exp(suffix + log_b), 0.0)
            acc = acc + _dot(w.astype(BF16), v_ref[krows, :])
            run = run + jnp.sum(lm, axis=1, keepdims=True)
            return run, acc

        init = (jnp.zeros((t, 1), F32), jnp.zeros((t, head_dim), F32))
        _, acc = lax.fori_loop(0, qb + 1, k_block, init)
        o_ref[qrows, :] = acc.astype(o_ref.dtype)
        return carry

    lax.fori_loop(0, nblk, q_block, 0)


def _stick_breaking(qkv, batch, seq, blk):
    m = batch * seq
    d = qkv.shape[1] // (3 * C_HEADS)
    return pl.pallas_call(
        functools.partial(_sb_kernel, blk=blk, nblk=seq // blk, head_dim=d),
        grid=(batch, C_HEADS),
        in_specs=[
            pl.BlockSpec((seq, d), lambda b, h: (b, h)),
            pl.BlockSpec((seq, d), lambda b, h: (b, C_HEADS + h)),
            pl.BlockSpec((seq, d), lambda b, h: (b, 2 * C_HEADS + h)),
        ],
        out_specs=pl.BlockSpec((seq, d), lambda b, h: (b, h)),
        out_shape=jax.ShapeDtypeStruct((m, C_HEADS * d), BF16),
        compiler_params=_params("parallel", "arbitrary"),
        name="stick_breaking",
    )(qkv, qkv, qkv)


def _permute_even_w_in(w):
    d = w.shape[0]
    qa, ca, qi, ki, wi, qb, kb, vb, gb, gk = jnp.split(
        w, [1024, 1152, 2176, 2240, 2256, 2768, 3280, 4304, 5328], axis=1)
    pad = jnp.zeros((d, LANES - IDX_DIM - IDX_HEADS - B_GATE_RANK), w.dtype)
    return jnp.concatenate([qa, qi, vb, gb, qb, kb, ca, ki, wi, gk, pad], axis=1)


def kernel(x, p, even_norm, even_w_in, even_w_gk, even_b_gk, even_gla_norm, even_w_out,
           odd_norm, odd_w_in, odd_w_out, ffn_norm, ffn_w_gate, ffn_w_up, ffn_w_down,
           ple_norm, ple_w_gate, ple_w_proj, final_norm):
    batch, seq, d = x.shape
    depth = p.shape[0]
    m = batch * seq
    h = x.reshape(m, d)
    tm = _tile(m, 1024)
    tm_ffn = _tile(m, 512)
    blk = _tile(seq, 256)

    for i in range(depth):
        j = i // 2
        if i % 2 == 0:
            w_in = _permute_even_w_in(even_w_in[j]).astype(BF16)
            u = _norm_matmul(h, even_norm[j], w_in, tm, _tile(U_WIDTH, 768))
            o_a = _dsa(u, batch, seq, blk)
            wgk_pad = jnp.zeros((LANES, B_HEADS * B_DK), F32)
            wgk_pad = wgk_pad.at[SM_GK:SM_GK + B_GATE_RANK].set(even_w_gk[j]).astype(BF16)
            o_b = _gla(u, wgk_pad, even_b_gk[j], even_gla_norm[j], batch, seq)
            w_out = even_w_out[j].astype(BF16)
            split = A_HEADS * A_LATENT
            h = _proj_residual(h, [o_a, o_b], [w_out[:split], w_out[split:]], tm, _tile(d, 512))
        else:
            qkv = _norm_matmul(h, odd_norm[j], odd_w_in[j].astype(BF16), tm, _tile(3 * d, 768))
            o = _stick_breaking(qkv, batch, seq, blk)
            h = _proj_residual(h, [o], [odd_w_out[j].astype(BF16)], tm, _tile(d, 512))
        h = _ffn(h, ffn_norm[i], ffn_w_gate[i].astype(BF16), ffn_w_up[i].astype(BF16),
                 ffn_w_down[i].astype(BF16), tm_ffn, _tile(ffn_w_gate.shape[2], 512))
        h = _ple(h, ple_norm[i], ple_w_gate[i].astype(BF16), p[i].reshape(m, -1),
                 ple_w_proj[i].astype(BF16), final_norm, tm_ffn, _tile(d, 512),
                 final_norm=(i == depth - 1))
    return h.reshape(batch, seq, d)
```

```python
import functools

import jax
import jax.numpy as jnp
from jax import lax
from jax.experimental import pallas as pl
from jax.experimental.pallas import tpu as pltpu

F32 = jnp.float32
BF16 = jnp.bfloat16
I32 = jnp.int32

EPS = 1e-6
CHUNK = 64
CHUNK_SHIFT = CHUNK.bit_length() - 1
A_HEADS = 8
A_LATENT = 128
A_TOPK_MAX = 256
IDX_HEADS = 16
IDX_DIM = 64
B_HEADS = 4
B_DK = 128
B_DV = 256
B_GATE_RANK = 16
B_GATE_NORM = 16.0
C_HEADS = 16
SB_HEADS_PER_STEP = 4
SB_EXP_UNDERFLOW = -104.0

LANES = 128
SUBLANES = 8
VMEM_LIMIT_BYTES = 56 * 1024 * 1024

U_QA = 0
U_QI = 1024
U_VB = 2048
U_GB = 3072
U_QB = 4096
U_KB = 4608
U_CA = 5120
U_SMALL = 5248
U_WIDTH = 5376
SM_W = 64
SM_GK = 80

NT_DIMS = (((1,), (1,)), ((), ()))
TN_DIMS = (((0,), (0,)), ((), ()))

KEY_NEG_INF = -2139095041
INT_MIN = -2147483648


def _params(*sem):
    return pltpu.CompilerParams(dimension_semantics=sem, vmem_limit_bytes=VMEM_LIMIT_BYTES)


def _tile(n, pref):
    if n <= pref:
        return n
    t = (pref // LANES) * LANES
    while n % t:
        t -= LANES
    return t


def _rms(x, g):
    ms = jnp.mean(x * x, axis=-1, keepdims=True)
    return x * lax.rsqrt(ms + EPS) * g


def _dot(a, b):
    return jnp.dot(a, b, preferred_element_type=F32)


def _dot_nt(a, b):
    return lax.dot_general(a, b, NT_DIMS, preferred_element_type=F32)


def _softplus_tail(z):
    return jnp.log1p(jnp.exp(-jnp.abs(z)))


def _split_bf16(x):
    hi = x.astype(BF16)
    lo = (x - hi.astype(F32)).astype(BF16)
    return hi, lo


def _norm_matmul_kernel(x_ref, g_ref, w_ref, o_ref, xn_ref):
    @pl.when(pl.program_id(1) == 0)
    def _():
        xn_ref[...] = _rms(x_ref[...], g_ref[...]).astype(BF16)

    o_ref[...] = _dot(xn_ref[...], w_ref[...]).astype(o_ref.dtype)


def _norm_matmul(x, g, w, tm, tn):
    m, k = x.shape
    n = w.shape[1]
    return pl.pallas_call(
        _norm_matmul_kernel,
        grid=(m // tm, n // tn),
        in_specs=[
            pl.BlockSpec((tm, k), lambda i, j: (i, 0)),
            pl.BlockSpec((1, k), lambda i, j: (0, 0)),
            pl.BlockSpec((k, tn), lambda i, j: (0, j)),
        ],
        out_specs=pl.BlockSpec((tm, tn), lambda i, j: (i, j)),
        out_shape=jax.ShapeDtypeStruct((m, n), BF16),
        scratch_shapes=[pltpu.VMEM((tm, k), BF16)],
        compiler_params=_params("parallel", "arbitrary"),
        name="norm_matmul",
    )(x, g.reshape(1, k), w)


def _proj_residual_kernel(*refs, n_in):
    h_ref = refs[0]
    a_refs = refs[1:1 + n_in]
    w_refs = refs[1 + n_in:1 + 2 * n_in]
    o_ref = refs[1 + 2 * n_in]
    acc = h_ref[...]
    for a_ref, w_ref in zip(a_refs, w_refs):
        acc = acc + _dot(a_ref[...], w_ref[...])
    o_ref[...] = acc


def _proj_residual(h, a_list, w_list, tm, tn):
    m, n = h.shape
    n_in = len(a_list)
    in_specs = [pl.BlockSpec((tm, tn), lambda i, j: (i, j))]
    in_specs += [pl.BlockSpec((tm, a.shape[1]), lambda i, j: (i, 0)) for a in a_list]
    in_specs += [pl.BlockSpec((w.shape[0], tn), lambda i, j: (0, j)) for w in w_list]
    return pl.pallas_call(
        functools.partial(_proj_residual_kernel, n_in=n_in),
        grid=(m // tm, n // tn),
        in_specs=in_specs,
        out_specs=pl.BlockSpec((tm, tn), lambda i, j: (i, j)),
        out_shape=jax.ShapeDtypeStruct((m, n), F32),
        compiler_params=_params("parallel", "arbitrary"),
        name="proj_residual",
    )(h, *a_list, *w_list)


def _ffn_kernel(x_ref, g_ref, wg_ref, wu_ref, wd_ref, o_ref, xn_ref):
    @pl.when(pl.program_id(1) == 0)
    def _():
        x = x_ref[...]
        xn_ref[...] = _rms(x, g_ref[...]).astype(BF16)
        o_ref[...] = x

    xn = xn_ref[...]
    gate = _dot(xn, wg_ref[...])
    up = _dot(xn, wu_ref[...])
    act = (gate * jax.nn.sigmoid(gate) * up).astype(BF16)
    o_ref[...] += _dot(act, wd_ref[...])


def _ffn(h, g, wg, wu, wd, tm, th):
    m, d = h.shape
    hid = wg.shape[1]
    return pl.pallas_call(
        _ffn_kernel,
        grid=(m // tm, hid // th),
        in_specs=[
            pl.BlockSpec((tm, d), lambda i, j: (i, 0)),
            pl.BlockSpec((1, d), lambda i, j: (0, 0)),
            pl.BlockSpec((d, th), lambda i, j: (0, j)),
            pl.BlockSpec((d, th), lambda i, j: (0, j)),
            pl.BlockSpec((th, d), lambda i, j: (j, 0)),
        ],
        out_specs=pl.BlockSpec((tm, d), lambda i, j: (i, 0)),
        out_shape=jax.ShapeDtypeStruct((m, d), F32),
        scratch_shapes=[pltpu.VMEM((tm, d), BF16)],
        compiler_params=_params("parallel", "arbitrary"),
        name="ffn",
    )(h, g.reshape(1, d), wg, wu, wd)


def _ple_kernel(x_ref, g_ref, wg_ref, p_ref, wp_ref, fg_ref, o_ref, xn_ref, *, tn, final_norm):
    j = pl.program_id(1)

    @pl.when(j == 0)
    def _():
        xn_ref[...] = _rms(x_ref[...], g_ref[...]).astype(BF16)

    cols = pl.ds(pl.multiple_of(j * tn, tn), tn)
    gate = jax.nn.sigmoid(_dot(xn_ref[...], wg_ref[...]))
    pe = _dot(p_ref[...].astype(BF16), wp_ref[...])
    o_ref[:, cols] = x_ref[:, cols] + pe * gate

    if final_norm:
        @pl.when(j == pl.num_programs(1) - 1)
        def _():
            o_ref[...] = _rms(o_ref[...], fg_ref[...])


def _ple(h, g, wg, p, wp, fg, tm, tn, final_norm):
    m, d = h.shape
    pd = p.shape[1]
    return pl.pallas_call(
        functools.partial(_ple_kernel, tn=tn, final_norm=final_norm),
        grid=(m // tm, d // tn),
        in_specs=[
            pl.BlockSpec((tm, d), lambda i, j: (i, 0)),
            pl.BlockSpec((1, d), lambda i, j: (0, 0)),
            pl.BlockSpec((d, tn), lambda i, j: (0, j)),
            pl.BlockSpec((tm, pd), lambda i, j: (i, 0)),
            pl.BlockSpec((pd, tn), lambda i, j: (0, j)),
            pl.BlockSpec((1, d), lambda i, j: (0, 0)),
        ],
        out_specs=pl.BlockSpec((tm, d), lambda i, j: (i, 0)),
        out_shape=jax.ShapeDtypeStruct((m, d), F32),
        scratch_shapes=[pltpu.VMEM((tm, d), BF16)],
        compiler_params=_params("parallel", "arbitrary"),
        name="ple",
    )(h, g.reshape(1, d), wg, p, wp, fg.reshape(1, d))


def _key_to_float(key):
    bits = key ^ ((key >> 31) & 0x7FFFFFFF)
    return lax.bitcast_convert_type(bits, F32)


def _dsa_kernel(qi_ref, qsm_ref, ksm_ref, qa_ref, c_ref, o_ref,
                sc_ref, ct_ref, acc_ref, m_ref, l_ref, *, topk, qb_size, seq):
    t = qb_size
    qb = pl.program_id(1)
    nk = qb + 1

    @pl.when(qb == 0)
    def _():
        ct_ref[...] = jnp.transpose(c_ref[...].astype(F32)).astype(BF16)

    row = lax.broadcasted_iota(I32, (t, t), 0)
    col = lax.broadcasted_iota(I32, (t, t), 1)
    limit = (((qb * t + col) >> CHUNK_SHIFT) + 1) << CHUNK_SHIFT

    w_t = jnp.transpose(qsm_ref[...].astype(F32))[SM_W:SM_W + IDX_HEADS, :]

    def score_tile(kt, carry):
        rows = pl.ds(pl.multiple_of(kt * t, t), t)
        kk = ksm_ref[rows, :][:, :IDX_DIM]
        sc = jnp.zeros((t, t), F32)
        for h in range(IDX_HEADS):
            qh = qi_ref[:, h * IDX_DIM:(h + 1) * IDX_DIM]
            lg = _dot_nt(kk, qh) * (IDX_DIM ** -0.5)
            sc = sc + jnp.maximum(lg, 0.0) * w_t[h:h + 1, :]
        sc = sc * (IDX_HEADS ** -0.5)
        sc_ref[rows, :] = jnp.where(kt * t + row < limit, sc, -jnp.inf)
        return carry

    lax.fori_loop(0, nk, score_tile, 0)

    def count(pred):
        def body(kt, acc):
            rows = pl.ds(pl.multiple_of(kt * t, t), t)
            hit = pred(sc_ref[rows, :]).astype(I32)
            return acc + jnp.sum(hit.reshape(t // SUBLANES, SUBLANES, t), axis=0)
        acc = lax.fori_loop(0, nk, body, jnp.zeros((SUBLANES, t), I32))
        return jnp.sum(acc, axis=0, keepdims=True)

    def count_ge(key):
        cand = _key_to_float(key)
        return count(lambda s: s >= cand)

    zero_key = jnp.zeros((1, t), I32)
    res0 = jnp.where(count_ge(zero_key) >= topk, 0, INT_MIN).astype(I32)

    def bisect(i, res):
        cand = res | jnp.left_shift(jnp.int32(1), 30 - i)
        return jnp.where(count_ge(cand) >= topk, cand, res)

    res = lax.fori_loop(0, 31, bisect, res0)
    thr = _key_to_float(jnp.maximum(res, KEY_NEG_INF))
    n_ge = count(lambda s: s >= thr) + jnp.where(thr == -jnp.inf, seq - nk * t, 0)

    @pl.when(jnp.max(n_ge) > topk)
    def _():
        need = (topk - count(lambda s: s > thr)).astype(F32)
        tri = (row >= col).astype(BF16)

        def body(kt, seen):
            rows = pl.ds(pl.multiple_of(kt * t, t), t)
            s = sc_ref[rows, :]
            eq = s == thr
            rank = _dot(tri, eq.astype(BF16)) + seen
            drop = (eq & (rank > need)) | (kt * t + row >= limit)
            sc_ref[rows, :] = jnp.where(drop, jnp.nan, s)
            return seen + jnp.sum(eq.astype(F32), axis=0, keepdims=True)

        lax.fori_loop(0, nk, body, jnp.zeros((1, t), F32))

    m_ref[...] = jnp.full(m_ref.shape, -1e30, F32)
    l_ref[...] = jnp.zeros(l_ref.shape, F32)
    acc_ref[...] = jnp.zeros(acc_ref.shape, F32)

    def attend(kt, carry):
        rows = pl.ds(pl.multiple_of(kt * t, t), t)
        valid = sc_ref[rows, :] >= thr
        ck = c_ref[rows, :]
        ckt = ct_ref[:, rows]
        for h in range(A_HEADS):
            qh = qa_ref[:, h * A_LATENT:(h + 1) * A_LATENT]
            s = jnp.where(valid, _dot_nt(ck, qh) * (A_LATENT ** -0.5), -1e30)
            m_old = m_ref[h:h + 1, :]
            m_new = jnp.maximum(m_old, jnp.max(s, axis=0, keepdims=True))
            alpha = jnp.exp(m_old - m_new)
            p = jnp.exp(s - m_new)
            l_ref[h:h + 1, :] = alpha * l_ref[h:h + 1, :] + jnp.sum(p, axis=0, keepdims=True)
            hs = slice(h * A_LATENT, (h + 1) * A_LATENT)
            acc_ref[hs, :] = alpha * acc_ref[hs, :] + _dot(ckt, p.astype(BF16))
            m_ref[h:h + 1, :] = m_new
        return carry

    lax.fori_loop(0, nk, attend, 0)

    for h in range(A_HEADS):
        hs = slice(h * A_LATENT, (h + 1) * A_LATENT)
        out = acc_ref[hs, :] / l_ref[h:h + 1, :]
        o_ref[:, hs] = jnp.transpose(out).astype(o_ref.dtype)


def _dsa(u, batch, seq, qb_size):
    m = batch * seq
    nqb = seq // qb_size
    topk = min(A_TOPK_MAX, seq // 4)
    wide = A_HEADS * A_LATENT
    return pl.pallas_call(
        functools.partial(_dsa_kernel, topk=topk, qb_size=qb_size, seq=seq),
        grid=(batch, nqb),
        in_specs=[
            pl.BlockSpec((qb_size, wide), lambda b, q: (b * nqb + q, U_QI // wide)),
            pl.BlockSpec((qb_size, LANES), lambda b, q: (b * nqb + q, U_SMALL // LANES)),
            pl.BlockSpec((seq, LANES), lambda b, q: (b, U_SMALL // LANES)),
            pl.BlockSpec((qb_size, wide), lambda b, q: (b * nqb + q, U_QA // wide)),
            pl.BlockSpec((seq, LANES), lambda b, q: (b, U_CA // LANES)),
        ],
        out_specs=pl.BlockSpec((qb_size, wide), lambda b, q: (b * nqb + q, 0)),
        out_shape=jax.ShapeDtypeStruct((m, wide), BF16),
        scratch_shapes=[
            pltpu.VMEM((seq, qb_size), F32),
            pltpu.VMEM((A_LATENT, seq), BF16),
            pltpu.VMEM((wide, qb_size), F32),
            pltpu.VMEM((A_HEADS, qb_size), F32),
            pltpu.VMEM((A_HEADS, qb_size), F32),
        ],
        compiler_params=_params("parallel", "arbitrary"),
        name="dsa",
    )(u, u, u, u, u)


def _gla_kernel(q_ref, k_ref, v_ref, g_ref, sm_ref, wgk_ref, bgk_ref, gn_ref, o_ref, st_ref,
                *, nchunks):
    c = CHUNK
    row = lax.broadcasted_iota(I32, (c, c), 0)
    col = lax.broadcasted_iota(I32, (c, c), 1)
    causal = row >= col
    tri = causal.astype(BF16)
    st_ref[...] = jnp.zeros(st_ref.shape, F32)
    wgk = wgk_ref[...]
    bgk = bgk_ref[...]
    gn = gn_ref[...]

    def chunk(ci, carry):
        rows = pl.ds(pl.multiple_of(ci * c, c), c)
        x = _dot(sm_ref[rows, :], wgk) + bgk
        gk = -(jnp.maximum(-x, 0.0) + _softplus_tail(x)) / B_GATE_NORM
        hi, lo = _split_bf16(gk)
        cum_all = _dot(tri, hi) + _dot(tri, lo)
        heads = range(B_HEADS)
        ksl = [slice(h * B_DK, (h + 1) * B_DK) for h in heads]
        vsl = [slice(h * B_DV, (h + 1) * B_DV) for h in heads]
        q_es, k_es, k_ds, decays = [], [], [], []
        for ks in ksl:
            cum = cum_all[:, ks]
            last = cum[c - 1:c, :]
            q = q_ref[rows, ks].astype(F32) * (B_DK ** -0.5)
            k = k_ref[rows, ks].astype(F32)
            q_es.append((q * jnp.exp(cum)).astype(BF16))
            k_es.append((k * jnp.exp(-cum)).astype(BF16))
            k_ds.append((k * jnp.exp(last - cum)).astype(BF16))
            decays.append(jnp.exp(last))
        vs_ = [v_ref[rows, vs] for vs in vsl]
        sts = [st_ref[h] for h in heads]
        atts = [_dot_nt(q_e, k_e) for q_e, k_e in zip(q_es, k_es)]
        inter = [_dot_nt(q_e, st.astype(BF16)) for q_e, st in zip(q_es, sts)]
        kvs = [lax.dot_general(v, k_d, TN_DIMS, preferred_element_type=F32)
               for v, k_d in zip(vs_, k_ds)]
        for h in heads:
            st_ref[h] = sts[h] * decays[h] + kvs[h]
        atts = [jnp.where(causal, a, 0.0).astype(BF16) for a in atts]
        intra = [_dot(a, v) for a, v in zip(atts, vs_)]
        for h in heads:
            gate = g_ref[rows, vsl[h]].astype(F32)
            o = intra[h] + inter[h]
            o_ref[rows, vsl[h]] = (_rms(o, gn) * (gate * jax.nn.sigmoid(gate))).astype(o_ref.dtype)
        return carry

    lax.fori_loop(0, nchunks, chunk, 0)


def _gla(u, wgk_pad, bgk, gnorm, batch, seq):
    m = batch * seq
    qk_wide = B_HEADS * B_DK
    v_wide = B_HEADS * B_DV
    return pl.pallas_call(
        functools.partial(_gla_kernel, nchunks=seq // CHUNK),
        grid=(batch,),
        in_specs=[
            pl.BlockSpec((seq, qk_wide), lambda b: (b, U_QB // qk_wide)),
            pl.BlockSpec((seq, qk_wide), lambda b: (b, U_KB // qk_wide)),
            pl.BlockSpec((seq, v_wide), lambda b: (b, U_VB // v_wide)),
            pl.BlockSpec((seq, v_wide), lambda b: (b, U_GB // v_wide)),
            pl.BlockSpec((seq, LANES), lambda b: (b, U_SMALL // LANES)),
            pl.BlockSpec((LANES, qk_wide), lambda b: (0, 0)),
            pl.BlockSpec((1, qk_wide), lambda b: (0, 0)),
            pl.BlockSpec((1, B_DV), lambda b: (0, 0)),
        ],
        out_specs=pl.BlockSpec((seq, v_wide), lambda b: (b, 0)),
        out_shape=jax.ShapeDtypeStruct((m, v_wide), BF16),
        scratch_shapes=[pltpu.VMEM((B_HEADS, B_DV, B_DK), F32)],
        compiler_params=_params("parallel"),
        name="gla",
    )(u, u, u, u, u, wgk_pad, bgk.reshape(1, -1), gnorm.reshape(1, -1))


def _sb_kernel(q_ref, k_ref, v_ref, o_ref, *, blk, nblk, head_dim, heads):
    t = blk
    d = head_dim
    row = lax.broadcasted_iota(I32, (t, t), 0)
    col = lax.broadcasted_iota(I32, (t, t), 1)
    upper = (row > col).astype(BF16)
    upper2 = jnp.concatenate([upper, upper], axis=0)
    diag = col < row
    scale = d ** -0.5

    hsl = [slice(h * d, (h + 1) * d) for h in range(heads)]

    def block(qrows, krows, runs, mask):
        zs = [_dot_nt(q_ref[qrows, hs], k_ref[krows, hs]) * scale for hs in hsl]
        lms, log_bs = [], []
        for z in zs:
            log_1mb = -(jnp.maximum(z, 0.0) + jnp.log(1.0 + jnp.exp(-jnp.abs(z))))
            log_bs.append(log_1mb + z)
            lms.append(log_1mb if mask is None else jnp.where(mask, log_1mb, 0.0))
        splits = [jnp.concatenate(_split_bf16(lm), axis=1) for lm in lms]
        sufs = [_dot(sp, upper2) for sp in splits]
        ws = []
        for suf, log_b, run in zip(sufs, log_bs, runs):
            w = jnp.exp(suf + run + log_b)
            ws.append((w if mask is None else jnp.where(mask, w, 0.0)).astype(BF16))
        outs = [_dot(w, v_ref[krows, hs]) for w, hs in zip(ws, hsl)]
        new_runs = [run + jnp.sum(lm, axis=1, keepdims=True) for run, lm in zip(runs, lms)]
        return new_runs, outs

    def q_block(qb, carry):
        qrows = pl.ds(pl.multiple_of(qb * t, t), t)
        runs, accs = block(qrows, qrows, [jnp.zeros((t, 1), F32)] * heads, diag)

        def alive(rs):
            top = jnp.max(rs[0])
            for r in rs[1:]:
                top = jnp.maximum(top, jnp.max(r))
            return (top >= SB_EXP_UNDERFLOW).astype(I32)

        def cond(state):
            i, live, _, _ = state
            return jnp.logical_and(i <= qb, live > 0)

        def k_block(state):
            i, _, rs, acs = state
            krows = pl.ds(pl.multiple_of((qb - i) * t, t), t)
            new_rs, outs = block(qrows, krows, list(rs), None)
            new_acs = [a + o for a, o in zip(acs, outs)]
            return i + 1, alive(new_rs), tuple(new_rs), tuple(new_acs)

        state = (jnp.int32(1), alive(runs), tuple(runs), tuple(accs))
        _, _, _, accs = lax.while_loop(cond, k_block, state)
        for h in range(heads):
            o_ref[qrows, h * d:(h + 1) * d] = accs[h].astype(o_ref.dtype)
        return carry

    lax.fori_loop(0, nblk, q_block, 0)


def _stick_breaking(qkv, batch, seq, blk):
    m = batch * seq
    d = qkv.shape[1] // (3 * C_HEADS)
    groups = C_HEADS // SB_HEADS_PER_STEP
    wide = SB_HEADS_PER_STEP * d
    return pl.pallas_call(
        functools.partial(_sb_kernel, blk=blk, nblk=seq // blk, head_dim=d, heads=SB_HEADS_PER_STEP),
        grid=(batch, groups),
        in_specs=[
            pl.BlockSpec((seq, wide), lambda b, g: (b, g)),
            pl.BlockSpec((seq, wide), lambda b, g: (b, groups + g)),
            pl.BlockSpec((seq, wide), lambda b, g: (b, 2 * groups + g)),
        ],
        out_specs=pl.BlockSpec((seq, wide), lambda b, g: (b, g)),
        out_shape=jax.ShapeDtypeStruct((m, C_HEADS * d), BF16),
        compiler_params=_params("parallel", "arbitrary"),
        name="stick_breaking",
    )(qkv, qkv, qkv)


def _permute_even_w_in(w):
    d = w.shape[0]
    qa, ca, qi, ki, wi, qb, kb, vb, gb, gk = jnp.split(
        w, [1024, 1152, 2176, 2240, 2256, 2768, 3280, 4304, 5328], axis=1)
    pad = jnp.zeros((d, LANES - IDX_DIM - IDX_HEADS - B_GATE_RANK), w.dtype)
    return jnp.concatenate([qa, qi, vb, gb, qb, kb, ca, ki, wi, gk, pad], axis=1)


def kernel(x, p, even_norm, even_w_in, even_w_gk, even_b_gk, even_gla_norm, even_w_out,
           odd_norm, odd_w_in, odd_w_out, ffn_norm, ffn_w_gate, ffn_w_up, ffn_w_down,
           ple_norm, ple_w_gate, ple_w_proj, final_norm):
    batch, seq, d = x.shape
    depth = p.shape[0]
    m = batch * seq
    h = x.reshape(m, d)
    tm = _tile(m, 1024)
    tm_ffn = _tile(m, 512)
    blk = _tile(seq, 256)

    for i in range(depth):
        j = i // 2
        if i % 2 == 0:
            w_in = _permute_even_w_in(even_w_in[j]).astype(BF16)
            u = _norm_matmul(h, even_norm[j], w_in, tm, _tile(U_WIDTH, 768))
            o_a = _dsa(u, batch, seq, blk)
            wgk_pad = jnp.zeros((LANES, B_HEADS * B_DK), F32)
            wgk_pad = wgk_pad.at[SM_GK:SM_GK + B_GATE_RANK].set(even_w_gk[j]).astype(BF16)
            o_b = _gla(u, wgk_pad, even_b_gk[j], even_gla_norm[j], batch, seq)
            w_out = even_w_out[j].astype(BF16)
            split = A_HEADS * A_LATENT
            h = _proj_residual(h, [o_a, o_b], [w_out[:split], w_out[split:]], tm, _tile(d, 512))
        else:
            qkv = _norm_matmul(h, odd_norm[j], odd_w_in[j].astype(BF16), tm, _tile(3 * d, 768))
            o = _stick_breaking(qkv, batch, seq, blk)
            h = _proj_residual(h, [o], [odd_w_out[j].astype(BF16)], tm, _tile(d, 512))
        h = _ffn(h, ffn_norm[i], ffn_w_gate[i].astype(BF16), ffn_w_up[i].astype(BF16),
                 ffn_w_down[i].astype(BF16), tm_ffn, _tile(ffn_w_gate.shape[2], 512))
        h = _ple(h, ple_norm[i], ple_w_gate[i].astype(BF16), p[i].reshape(m, -1),
                 ple_w_proj[i].astype(BF16), final_norm, tm_ffn, _tile(d, 512),
                 final_norm=(i == depth - 1))
    return h.reshape(batch, seq, d)
```

```python
import functools

import jax
import jax.numpy as jnp
from jax import lax
from jax.experimental import pallas as pl
from jax.experimental.pallas import tpu as pltpu

F32 = jnp.float32
BF16 = jnp.bfloat16
I32 = jnp.int32

EPS = 1e-6
CHUNK = 64
CHUNK_SHIFT = CHUNK.bit_length() - 1
A_HEADS = 8
A_LATENT = 128
A_TOPK_MAX = 256
IDX_HEADS = 16
IDX_DIM = 64
B_HEADS = 4
B_DK = 128
B_DV = 256
B_GATE_RANK = 16
B_GATE_NORM = 16.0
C_HEADS = 16
DSA_HEAD_GROUP = 4
LOG2_E = 1.4426950408889634
SB_HEADS_PER_STEP = 4
SB_EXP_UNDERFLOW = -104.0

LANES = 128
SUBLANES = 8
VMEM_LIMIT_BYTES = 56 * 1024 * 1024

U_QA = 0
U_QI = 1024
U_VB = 2048
U_GB = 3072
U_QB = 4096
U_KB = 4608
U_CA = 5120
U_SMALL = 5248
U_WIDTH = 5376
SM_W = 64
SM_GK = 80

NT_DIMS = (((1,), (1,)), ((), ()))
TN_DIMS = (((0,), (0,)), ((), ()))

KEY_NEG_INF = -2139095041
INT_MIN = -2147483648


def _params(*sem):
    return pltpu.CompilerParams(dimension_semantics=sem, vmem_limit_bytes=VMEM_LIMIT_BYTES)


def _tile(n, pref):
    if n <= pref:
        return n
    t = (pref // LANES) * LANES
    while n % t:
        t -= LANES
    return t


def _rms(x, g):
    ms = jnp.mean(x * x, axis=-1, keepdims=True)
    return x * lax.rsqrt(ms + EPS) * g


def _dot(a, b):
    return jnp.dot(a, b, preferred_element_type=F32)


def _dot_nt(a, b):
    return lax.dot_general(a, b, NT_DIMS, preferred_element_type=F32)


def _softplus_tail(z):
    return jnp.log1p(jnp.exp(-jnp.abs(z)))


def _split_bf16(x):
    hi = x.astype(BF16)
    lo = (x - hi.astype(F32)).astype(BF16)
    return hi, lo


def _norm_matmul_kernel(x_ref, g_ref, w_ref, o_ref, xn_ref):
    @pl.when(pl.program_id(1) == 0)
    def _():
        xn_ref[...] = _rms(x_ref[...], g_ref[...]).astype(BF16)

    o_ref[...] = _dot(xn_ref[...], w_ref[...]).astype(o_ref.dtype)


def _norm_matmul(x, g, w, tm, tn):
    m, k = x.shape
    n = w.shape[1]
    return pl.pallas_call(
        _norm_matmul_kernel,
        grid=(m // tm, n // tn),
        in_specs=[
            pl.BlockSpec((tm, k), lambda i, j: (i, 0)),
            pl.BlockSpec((1, k), lambda i, j: (0, 0)),
            pl.BlockSpec((k, tn), lambda i, j: (0, j)),
        ],
        out_specs=pl.BlockSpec((tm, tn), lambda i, j: (i, j)),
        out_shape=jax.ShapeDtypeStruct((m, n), BF16),
        scratch_shapes=[pltpu.VMEM((tm, k), BF16)],
        compiler_params=_params("parallel", "arbitrary"),
        name="norm_matmul",
    )(x, g.reshape(1, k), w)


def _proj_residual_kernel(*refs, n_in):
    h_ref = refs[0]
    a_refs = refs[1:1 + n_in]
    w_refs = refs[1 + n_in:1 + 2 * n_in]
    o_ref = refs[1 + 2 * n_in]
    acc = h_ref[...]
    for a_ref, w_ref in zip(a_refs, w_refs):
        acc = acc + _dot(a_ref[...], w_ref[...])
    o_ref[...] = acc


def _proj_residual(h, a_list, w_list, tm, tn):
    m, n = h.shape
    n_in = len(a_list)
    in_specs = [pl.BlockSpec((tm, tn), lambda i, j: (i, j))]
    in_specs += [pl.BlockSpec((tm, a.shape[1]), lambda i, j: (i, 0)) for a in a_list]
    in_specs += [pl.BlockSpec((w.shape[0], tn), lambda i, j: (0, j)) for w in w_list]
    return pl.pallas_call(
        functools.partial(_proj_residual_kernel, n_in=n_in),
        grid=(m // tm, n // tn),
        in_specs=in_specs,
        out_specs=pl.BlockSpec((tm, tn), lambda i, j: (i, j)),
        out_shape=jax.ShapeDtypeStruct((m, n), F32),
        compiler_params=_params("parallel", "arbitrary"),
        name="proj_residual",
    )(h, *a_list, *w_list)


def _ffn_kernel(x_ref, g_ref, wg_ref, wu_ref, wd_ref, o_ref, xn_ref):
    @pl.when(pl.program_id(1) == 0)
    def _():
        x = x_ref[...]
        xn_ref[...] = _rms(x, g_ref[...]).astype(BF16)
        o_ref[...] = x

    xn = xn_ref[...]
    gate = _dot(xn, wg_ref[...])
    up = _dot(xn, wu_ref[...])
    act = (gate * jax.nn.sigmoid(gate) * up).astype(BF16)
    o_ref[...] += _dot(act, wd_ref[...])


def _ffn(h, g, wg, wu, wd, tm, th):
    m, d = h.shape
    hid = wg.shape[1]
    return pl.pallas_call(
        _ffn_kernel,
        grid=(m // tm, hid // th),
        in_specs=[
            pl.BlockSpec((tm, d), lambda i, j: (i, 0)),
            pl.BlockSpec((1, d), lambda i, j: (0, 0)),
            pl.BlockSpec((d, th), lambda i, j: (0, j)),
            pl.BlockSpec((d, th), lambda i, j: (0, j)),
            pl.BlockSpec((th, d), lambda i, j: (j, 0)),
        ],
        out_specs=pl.BlockSpec((tm, d), lambda i, j: (i, 0)),
        out_shape=jax.ShapeDtypeStruct((m, d), F32),
        scratch_shapes=[pltpu.VMEM((tm, d), BF16)],
        compiler_params=_params("parallel", "arbitrary"),
        name="ffn",
    )(h, g.reshape(1, d), wg, wu, wd)


def _ple_kernel(x_ref, g_ref, wg_ref, p_ref, wp_ref, fg_ref, o_ref, xn_ref, *, tn, final_norm):
    j = pl.program_id(1)

    @pl.when(j == 0)
    def _():
        xn_ref[...] = _rms(x_ref[...], g_ref[...]).astype(BF16)

    cols = pl.ds(pl.multiple_of(j * tn, tn), tn)
    gate = jax.nn.sigmoid(_dot(xn_ref[...], wg_ref[...]))
    pe = _dot(p_ref[...].astype(BF16), wp_ref[...])
    o_ref[:, cols] = x_ref[:, cols] + pe * gate

    if final_norm:
        @pl.when(j == pl.num_programs(1) - 1)
        def _():
            o_ref[...] = _rms(o_ref[...], fg_ref[...])


def _ple(h, g, wg, p, wp, fg, tm, tn, final_norm):
    m, d = h.shape
    pd = p.shape[1]
    return pl.pallas_call(
        functools.partial(_ple_kernel, tn=tn, final_norm=final_norm),
        grid=(m // tm, d // tn),
        in_specs=[
            pl.BlockSpec((tm, d), lambda i, j: (i, 0)),
            pl.BlockSpec((1, d), lambda i, j: (0, 0)),
            pl.BlockSpec((d, tn), lambda i, j: (0, j)),
            pl.BlockSpec((tm, pd), lambda i, j: (i, 0)),
            pl.BlockSpec((pd, tn), lambda i, j: (0, j)),
            pl.BlockSpec((1, d), lambda i, j: (0, 0)),
        ],
        out_specs=pl.BlockSpec((tm, d), lambda i, j: (i, 0)),
        out_shape=jax.ShapeDtypeStruct((m, d), F32),
        scratch_shapes=[pltpu.VMEM((tm, d), BF16)],
        compiler_params=_params("parallel", "arbitrary"),
        name="ple",
    )(h, g.reshape(1, d), wg, p, wp, fg.reshape(1, d))


def _key_to_float(key):
    bits = key ^ ((key >> 31) & 0x7FFFFFFF)
    return lax.bitcast_convert_type(bits, F32)


def _dsa_kernel(qi_ref, qsm_ref, ksm_ref, qa_ref, c_ref, o_ref,
                sc_ref, ct_ref, acc_ref, m_ref, l_ref, *, topk, qb_size, seq):
    t = qb_size
    qb = pl.program_id(1)
    nk = qb + 1

    @pl.when(qb == 0)
    def _():
        ct_ref[...] = jnp.transpose(c_ref[...].astype(F32)).astype(BF16)

    row = lax.broadcasted_iota(I32, (t, t), 0)
    col = lax.broadcasted_iota(I32, (t, t), 1)
    limit = (((qb * t + col) >> CHUNK_SHIFT) + 1) << CHUNK_SHIFT

    w_t = jnp.transpose(qsm_ref[...].astype(F32))[SM_W:SM_W + IDX_HEADS, :]

    def score_tile(kt, carry):
        rows = pl.ds(pl.multiple_of(kt * t, t), t)
        kk = ksm_ref[rows, :][:, :IDX_DIM]
        sc = jnp.zeros((t, t), F32)
        for h in range(IDX_HEADS):
            qh = qi_ref[:, h * IDX_DIM:(h + 1) * IDX_DIM]
            sc = sc + jnp.maximum(_dot_nt(kk, qh), 0.0) * w_t[h:h + 1, :]
        sc = sc * (IDX_DIM ** -0.5 * IDX_HEADS ** -0.5)
        sc_ref[rows, :] = jnp.where(kt * t + row < limit, sc, -jnp.inf)
        return carry

    lax.fori_loop(0, nk, score_tile, 0)

    def count(pred):
        def body(kt, acc):
            rows = pl.ds(pl.multiple_of(kt * t, t), t)
            hit = pred(sc_ref[rows, :]).astype(I32)
            return acc + jnp.sum(hit.reshape(t // SUBLANES, SUBLANES, t), axis=0)
        acc = lax.fori_loop(0, nk, body, jnp.zeros((SUBLANES, t), I32))
        return jnp.sum(acc, axis=0, keepdims=True)

    def count_ge(key):
        cand = _key_to_float(key)
        return count(lambda s: s >= cand)

    zero_key = jnp.zeros((1, t), I32)
    res0 = jnp.where(count_ge(zero_key) >= topk, 0, INT_MIN).astype(I32)

    def bisect(i, res):
        cand = res | jnp.left_shift(jnp.int32(1), 30 - i)
        return jnp.where(count_ge(cand) >= topk, cand, res)

    res = lax.fori_loop(0, 31, bisect, res0)
    thr = _key_to_float(jnp.maximum(res, KEY_NEG_INF))
    n_ge = count(lambda s: s >= thr) + jnp.where(thr == -jnp.inf, seq - nk * t, 0)

    @pl.when(jnp.max(n_ge) > topk)
    def _():
        need = (topk - count(lambda s: s > thr)).astype(F32)
        tri = (row >= col).astype(BF16)

        def body(kt, seen):
            rows = pl.ds(pl.multiple_of(kt * t, t), t)
            s = sc_ref[rows, :]
            eq = s == thr
            rank = _dot(tri, eq.astype(BF16)) + seen
            drop = (eq & (rank > need)) | (kt * t + row >= limit)
            sc_ref[rows, :] = jnp.where(drop, jnp.nan, s)
            return seen + jnp.sum(eq.astype(F32), axis=0, keepdims=True)

        lax.fori_loop(0, nk, body, jnp.zeros((1, t), F32))

    m_ref[...] = jnp.full(m_ref.shape, -1e30, F32)
    l_ref[...] = jnp.zeros(l_ref.shape, F32)
    acc_ref[...] = jnp.zeros(acc_ref.shape, F32)

    def attend(kt, carry):
        rows = pl.ds(pl.multiple_of(kt * t, t), t)
        valid = sc_ref[rows, :] >= thr
        ck = c_ref[rows, :]
        ckt = ct_ref[:, rows]
        c2 = (A_LATENT ** -0.5) * LOG2_E
        for g in range(0, A_HEADS, DSA_HEAD_GROUP):
            group = range(g, g + DSA_HEAD_GROUP)
            raws = [jnp.where(valid, _dot_nt(ck, qa_ref[:, h * A_LATENT:(h + 1) * A_LATENT]), -1e30)
                    for h in group]
            ps, alphas = [], []
            for h, raw in zip(group, raws):
                m_old = m_ref[h:h + 1, :]
                m_new = jnp.maximum(m_old, jnp.max(raw, axis=0, keepdims=True))
                alpha = jnp.exp2((m_old - m_new) * c2)
                p = jnp.exp2((raw - m_new) * c2)
                l_ref[h:h + 1, :] = alpha * l_ref[h:h + 1, :] + jnp.sum(p, axis=0, keepdims=True)
                m_ref[h:h + 1, :] = m_new
                ps.append(p.astype(BF16))
                alphas.append(alpha)
            pvs = [_dot(ckt, p) for p in ps]
            for h, alpha, pv in zip(group, alphas, pvs):
                hs = slice(h * A_LATENT, (h + 1) * A_LATENT)
                acc_ref[hs, :] = alpha * acc_ref[hs, :] + pv
        return carry

    lax.fori_loop(0, nk, attend, 0)

    for h in range(A_HEADS):
        hs = slice(h * A_LATENT, (h + 1) * A_LATENT)
        out = acc_ref[hs, :] / l_ref[h:h + 1, :]
        o_ref[:, hs] = jnp.transpose(out).astype(o_ref.dtype)


def _dsa(u, batch, seq, qb_size):
    m = batch * seq
    nqb = seq // qb_size
    topk = min(A_TOPK_MAX, seq // 4)
    wide = A_HEADS * A_LATENT
    return pl.pallas_call(
        functools.partial(_dsa_kernel, topk=topk, qb_size=qb_size, seq=seq),
        grid=(batch, nqb),
        in_specs=[
            pl.BlockSpec((qb_size, wide), lambda b, q: (b * nqb + q, U_QI // wide)),
            pl.BlockSpec((qb_size, LANES), lambda b, q: (b * nqb + q, U_SMALL // LANES)),
            pl.BlockSpec((seq, LANES), lambda b, q: (b, U_SMALL // LANES)),
            pl.BlockSpec((qb_size, wide), lambda b, q: (b * nqb + q, U_QA // wide)),
            pl.BlockSpec((seq, LANES), lambda b, q: (b, U_CA // LANES)),
        ],
        out_specs=pl.BlockSpec((qb_size, wide), lambda b, q: (b * nqb + q, 0)),
        out_shape=jax.ShapeDtypeStruct((m, wide), BF16),
        scratch_shapes=[
            pltpu.VMEM((seq, qb_size), F32),
            pltpu.VMEM((A_LATENT, seq), BF16),
            pltpu.VMEM((wide, qb_size), F32),
            pltpu.VMEM((A_HEADS, qb_size), F32),
            pltpu.VMEM((A_HEADS, qb_size), F32),
        ],
        compiler_params=_params("parallel", "arbitrary"),
        name="dsa",
    )(u, u, u, u, u)


def _gla_kernel(q_ref, k_ref, v_ref, g_ref, sm_ref, wgk_ref, bgk_ref, gn_ref, o_ref, st_ref,
                *, nchunks):
    c = CHUNK
    row = lax.broadcasted_iota(I32, (c, c), 0)
    col = lax.broadcasted_iota(I32, (c, c), 1)
    causal = row >= col
    tri = causal.astype(BF16)
    st_ref[...] = jnp.zeros(st_ref.shape, F32)
    wgk = wgk_ref[...]
    bgk = bgk_ref[...]
    gn = gn_ref[...]

    def chunk(ci, carry):
        rows = pl.ds(pl.multiple_of(ci * c, c), c)
        x = _dot(sm_ref[rows, :], wgk) + bgk
        gk = -(jnp.maximum(-x, 0.0) + _softplus_tail(x)) / B_GATE_NORM
        hi, lo = _split_bf16(gk)
        cum_all = _dot(tri, hi) + _dot(tri, lo)
        heads = range(B_HEADS)
        ksl = [slice(h * B_DK, (h + 1) * B_DK) for h in heads]
        vsl = [slice(h * B_DV, (h + 1) * B_DV) for h in heads]
        q_es, k_es, k_ds, decays = [], [], [], []
        for ks in ksl:
            cum = cum_all[:, ks]
            last = cum[c - 1:c, :]
            q = q_ref[rows, ks].astype(F32) * (B_DK ** -0.5)
            k = k_ref[rows, ks].astype(F32)
            q_es.append((q * jnp.exp(cum)).astype(BF16))
            k_es.append((k * jnp.exp(-cum)).astype(BF16))
            k_ds.append((k * jnp.exp(last - cum)).astype(BF16))
            decays.append(jnp.exp(last))
        vs_ = [v_ref[rows, vs] for vs in vsl]
        sts = [st_ref[h] for h in heads]
        atts = [_dot_nt(q_e, k_e) for q_e, k_e in zip(q_es, k_es)]
        inter = [_dot_nt(q_e, st.astype(BF16)) for q_e, st in zip(q_es, sts)]
        kvs = [lax.dot_general(v, k_d, TN_DIMS, preferred_element_type=F32)
               for v, k_d in zip(vs_, k_ds)]
        for h in heads:
            st_ref[h] = sts[h] * decays[h] + kvs[h]
        atts = [jnp.where(causal, a, 0.0).astype(BF16) for a in atts]
        intra = [_dot(a, v) for a, v in zip(atts, vs_)]
        for h in heads:
            gate = g_ref[rows, vsl[h]].astype(F32)
            o = intra[h] + inter[h]
            o_ref[rows, vsl[h]] = (_rms(o, gn) * (gate * jax.nn.sigmoid(gate))).astype(o_ref.dtype)
        return carry

    lax.fori_loop(0, nchunks, chunk, 0)


def _gla(u, wgk_pad, bgk, gnorm, batch, seq):
    m = batch * seq
    qk_wide = B_HEADS * B_DK
    v_wide = B_HEADS * B_DV
    return pl.pallas_call(
        functools.partial(_gla_kernel, nchunks=seq // CHUNK),
        grid=(batch,),
        in_specs=[
            pl.BlockSpec((seq, qk_wide), lambda b: (b, U_QB // qk_wide)),
            pl.BlockSpec((seq, qk_wide), lambda b: (b, U_KB // qk_wide)),
            pl.BlockSpec((seq, v_wide), lambda b: (b, U_VB // v_wide)),
            pl.BlockSpec((seq, v_wide), lambda b: (b, U_GB // v_wide)),
            pl.BlockSpec((seq, LANES), lambda b: (b, U_SMALL // LANES)),
            pl.BlockSpec((LANES, qk_wide), lambda b: (0, 0)),
            pl.BlockSpec((1, qk_wide), lambda b: (0, 0)),
            pl.BlockSpec((1, B_DV), lambda b: (0, 0)),
        ],
        out_specs=pl.BlockSpec((seq, v_wide), lambda b: (b, 0)),
        out_shape=jax.ShapeDtypeStruct((m, v_wide), BF16),
        scratch_shapes=[pltpu.VMEM((B_HEADS, B_DV, B_DK), F32)],
        compiler_params=_params("parallel"),
        name="gla",
    )(u, u, u, u, u, wgk_pad, bgk.reshape(1, -1), gnorm.reshape(1, -1))


def _sb_kernel(q_ref, k_ref, v_ref, o_ref, *, blk, nblk, head_dim, heads):
    t = blk
    d = head_dim
    row = lax.broadcasted_iota(I32, (t, t), 0)
    col = lax.broadcasted_iota(I32, (t, t), 1)
    upper = (row > col).astype(BF16)
    upper2 = jnp.concatenate([upper, upper], axis=0)
    diag = col < row
    scale = d ** -0.5

    hsl = [slice(h * d, (h + 1) * d) for h in range(heads)]

    def block(qrows, krows, runs, mask):
        zs = [_dot_nt(q_ref[qrows, hs], k_ref[krows, hs]) * scale for hs in hsl]
        lms, log_bs = [], []
        for z in zs:
            log_1mb = -(jnp.maximum(z, 0.0) + jnp.log(1.0 + jnp.exp(-jnp.abs(z))))
            log_bs.append(log_1mb + z)
            lms.append(log_1mb if mask is None else jnp.where(mask, log_1mb, 0.0))
        splits = [jnp.concatenate(_split_bf16(lm), axis=1) for lm in lms]
        sufs = [_dot(sp, upper2) for sp in splits]
        ws = []
        for suf, log_b, run in zip(sufs, log_bs, runs):
            w = jnp.exp(suf + run + log_b)
            ws.append((w if mask is None else jnp.where(mask, w, 0.0)).astype(BF16))
        outs = [_dot(w, v_ref[krows, hs]) for w, hs in zip(ws, hsl)]
        new_runs = [run + jnp.sum(lm, axis=1, keepdims=True) for run, lm in zip(runs, lms)]
        return new_runs, outs

    def q_block(qb, carry):
        qrows = pl.ds(pl.multiple_of(qb * t, t), t)
        runs, accs = block(qrows, qrows, [jnp.zeros((t, 1), F32)] * heads, diag)

        def alive(rs):
            top = jnp.max(rs[0])
            for r in rs[1:]:
                top = jnp.maximum(top, jnp.max(r))
            return (top >= SB_EXP_UNDERFLOW).astype(I32)

        def cond(state):
            i, live, _, _ = state
            return jnp.logical_and(i <= qb, live > 0)

        def k_block(state):
            i, _, rs, acs = state
            krows = pl.ds(pl.multiple_of((qb - i) * t, t), t)
            new_rs, outs = block(qrows, krows, list(rs), None)
            new_acs = [a + o for a, o in zip(acs, outs)]
            return i + 1, alive(new_rs), tuple(new_rs), tuple(new_acs)

        state = (jnp.int32(1), alive(runs), tuple(runs), tuple(accs))
        _, _, _, accs = lax.while_loop(cond, k_block, state)
        for h in range(heads):
            o_ref[qrows, h * d:(h + 1) * d] = accs[h].astype(o_ref.dtype)
        return carry

    lax.fori_loop(0, nblk, q_block, 0)


def _stick_breaking(qkv, batch, seq, blk):
    m = batch * seq
    d = qkv.shape[1] // (3 * C_HEADS)
    groups = C_HEADS // SB_HEADS_PER_STEP
    wide = SB_HEADS_PER_STEP * d
    return pl.pallas_call(
        functools.partial(_sb_kernel, blk=blk, nblk=seq // blk, head_dim=d, heads=SB_HEADS_PER_STEP),
        grid=(batch, groups),
        in_specs=[
            pl.BlockSpec((seq, wide), lambda b, g: (b, g)),
            pl.BlockSpec((seq, wide), lambda b, g: (b, groups + g)),
            pl.BlockSpec((seq, wide), lambda b, g: (b, 2 * groups + g)),
        ],
        out_specs=pl.BlockSpec((seq, wide), lambda b, g: (b, g)),
        out_shape=jax.ShapeDtypeStruct((m, C_HEADS * d), BF16),
        compiler_params=_params("parallel", "arbitrary"),
        name="stick_breaking",
    )(qkv, qkv, qkv)


def _permute_even_w_in(w):
    d = w.shape[0]
    qa, ca, qi, ki, wi, qb, kb, vb, gb, gk = jnp.split(
        w, [1024, 1152, 2176, 2240, 2256, 2768, 3280, 4304, 5328], axis=1)
    pad = jnp.zeros((d, LANES - IDX_DIM - IDX_HEADS - B_GATE_RANK), w.dtype)
    return jnp.concatenate([qa, qi, vb, gb, qb, kb, ca, ki, wi, gk, pad], axis=1)


def kernel(x, p, even_norm, even_w_in, even_w_gk, even_b_gk, even_gla_norm, even_w_out,
           odd_norm, odd_w_in, odd_w_out, ffn_norm, ffn_w_gate, ffn_w_up, ffn_w_down,
           ple_norm, ple_w_gate, ple_w_proj, final_norm):
    batch, seq, d = x.shape
    depth = p.shape[0]
    m = batch * seq
    h = x.reshape(m, d)
    tm = _tile(m, 1024)
    tm_ffn = _tile(m, 512)
    blk = _tile(seq, 256)

    for i in range(depth):
        j = i // 2
        if i % 2 == 0:
            w_in = _permute_even_w_in(even_w_in[j]).astype(BF16)
            u = _norm_matmul(h, even_norm[j], w_in, tm, _tile(U_WIDTH, 768))
            o_a = _dsa(u, batch, seq, blk)
            wgk_pad = jnp.zeros((LANES, B_HEADS * B_DK), F32)
            wgk_pad = wgk_pad.at[SM_GK:SM_GK + B_GATE_RANK].set(even_w_gk[j]).astype(BF16)
            o_b = _gla(u, wgk_pad, even_b_gk[j], even_gla_norm[j], batch, seq)
            w_out = even_w_out[j].astype(BF16)
            split = A_HEADS * A_LATENT
            h = _proj_residual(h, [o_a, o_b], [w_out[:split], w_out[split:]], tm, _tile(d, 512))
        else:
            qkv = _norm_matmul(h, odd_norm[j], odd_w_in[j].astype(BF16), tm, _tile(3 * d, 768))
            o = _stick_breaking(qkv, batch, seq, blk)
            h = _proj_residual(h, [o], [odd_w_out[j].astype(BF16)], tm, _tile(d, 512))
        h = _ffn(h, ffn_norm[i], ffn_w_gate[i].astype(BF16), ffn_w_up[i].astype(BF16),
                 ffn_w_down[i].astype(BF16), tm_ffn, _tile(ffn_w_gate.shape[2], 512))
        h = _ple(h, ple_norm[i], ple_w_gate[i].astype(BF16), p[i].reshape(m, -1),
                 ple_w_proj[i].astype(BF16), final_norm, tm_ffn, _tile(d, 512),
                 final_norm=(i == depth - 1))
    return h.reshape(batch, seq, d)
```

```python
import functools

import jax
import jax.numpy as jnp
from jax import lax
from jax.experimental import pallas as pl
from jax.experimental.pallas import tpu as pltpu

F32 = jnp.float32
BF16 = jnp.bfloat16
I32 = jnp.int32

EPS = 1e-6
CHUNK = 64
CHUNK_SHIFT = CHUNK.bit_length() - 1
A_HEADS = 8
A_LATENT = 128
A_TOPK_MAX = 256
IDX_HEADS = 16
IDX_DIM = 64
B_HEADS = 4
B_DK = 128
B_DV = 256
B_GATE_RANK = 16
B_GATE_NORM = 16.0
C_HEADS = 16
DSA_HEAD_GROUP = 4
LOG2_E = 1.4426950408889634
SB_HEADS_PER_STEP = 4
SB_EXP_UNDERFLOW = -104.0

LANES = 128
SUBLANES = 8
VMEM_LIMIT_BYTES = 56 * 1024 * 1024

U_QA = 0
U_QI = 1024
U_VB = 2048
U_GB = 3072
U_QB = 4096
U_KB = 4608
U_CA = 5120
U_SMALL = 5248
U_WIDTH = 5376
SM_W = 64
SM_GK = 80

NT_DIMS = (((1,), (1,)), ((), ()))
TN_DIMS = (((0,), (0,)), ((), ()))

KEY_NEG_INF = -2139095041
INT_MIN = -2147483648


def _params(*sem):
    return pltpu.CompilerParams(dimension_semantics=sem, vmem_limit_bytes=VMEM_LIMIT_BYTES)


def _tile(n, pref):
    if n <= pref:
        return n
    t = (pref // LANES) * LANES
    while n % t:
        t -= LANES
    return t


def _rms(x, g):
    ms = jnp.mean(x * x, axis=-1, keepdims=True)
    return x * lax.rsqrt(ms + EPS) * g


def _dot(a, b):
    return jnp.dot(a, b, preferred_element_type=F32)


def _dot_nt(a, b):
    return lax.dot_general(a, b, NT_DIMS, preferred_element_type=F32)


def _softplus_tail(z):
    return jnp.log1p(jnp.exp(-jnp.abs(z)))


def _split_bf16(x):
    hi = x.astype(BF16)
    lo = (x - hi.astype(F32)).astype(BF16)
    return hi, lo


def _norm_matmul_kernel(x_ref, g_ref, w_ref, o_ref, xn_ref):
    @pl.when(pl.program_id(1) == 0)
    def _():
        xn_ref[...] = _rms(x_ref[...], g_ref[...]).astype(BF16)

    o_ref[...] = _dot(xn_ref[...], w_ref[...]).astype(o_ref.dtype)


def _norm_matmul(x, g, w, tm, tn):
    m, k = x.shape
    n = w.shape[1]
    return pl.pallas_call(
        _norm_matmul_kernel,
        grid=(m // tm, n // tn),
        in_specs=[
            pl.BlockSpec((tm, k), lambda i, j: (i, 0)),
            pl.BlockSpec((1, k), lambda i, j: (0, 0)),
            pl.BlockSpec((k, tn), lambda i, j: (0, j)),
        ],
        out_specs=pl.BlockSpec((tm, tn), lambda i, j: (i, j)),
        out_shape=jax.ShapeDtypeStruct((m, n), BF16),
        scratch_shapes=[pltpu.VMEM((tm, k), BF16)],
        compiler_params=_params("parallel", "arbitrary"),
        name="norm_matmul",
    )(x, g.reshape(1, k), w)


def _proj_residual_kernel(*refs, n_in):
    h_ref = refs[0]
    a_refs = refs[1:1 + n_in]
    w_refs = refs[1 + n_in:1 + 2 * n_in]
    o_ref = refs[1 + 2 * n_in]
    acc = h_ref[...]
    for a_ref, w_ref in zip(a_refs, w_refs):
        acc = acc + _dot(a_ref[...], w_ref[...])
    o_ref[...] = acc


def _resident(shape):
    return pl.BlockSpec(shape, lambda i: (0,) * len(shape), pipeline_mode=pl.Buffered(1))


def _proj_residual(h, a_list, w_list, tm):
    m, n = h.shape
    n_in = len(a_list)
    in_specs = [pl.BlockSpec((tm, n), lambda i: (i, 0))]
    in_specs += [pl.BlockSpec((tm, a.shape[1]), lambda i: (i, 0)) for a in a_list]
    in_specs += [_resident(w.shape) for w in w_list]
    return pl.pallas_call(
        functools.partial(_proj_residual_kernel, n_in=n_in),
        grid=(m // tm,),
        in_specs=in_specs,
        out_specs=pl.BlockSpec((tm, n), lambda i: (i, 0)),
        out_shape=jax.ShapeDtypeStruct((m, n), F32),
        compiler_params=_params("parallel"),
        name="proj_residual",
    )(h, *a_list, *w_list)


def _ffn_kernel(x_ref, g_ref, wg_ref, wu_ref, wd_ref, o_ref, xn_ref):
    @pl.when(pl.program_id(1) == 0)
    def _():
        x = x_ref[...]
        xn_ref[...] = _rms(x, g_ref[...]).astype(BF16)
        o_ref[...] = x

    xn = xn_ref[...]
    gate = _dot(xn, wg_ref[...])
    up = _dot(xn, wu_ref[...])
    act = (gate * jax.nn.sigmoid(gate) * up).astype(BF16)
    o_ref[...] += _dot(act, wd_ref[...])


def _ffn(h, g, wg, wu, wd, tm, th):
    m, d = h.shape
    hid = wg.shape[1]
    return pl.pallas_call(
        _ffn_kernel,
        grid=(m // tm, hid // th),
        in_specs=[
            pl.BlockSpec((tm, d), lambda i, j: (i, 0)),
            pl.BlockSpec((1, d), lambda i, j: (0, 0)),
            pl.BlockSpec((d, th), lambda i, j: (0, j)),
            pl.BlockSpec((d, th), lambda i, j: (0, j)),
            pl.BlockSpec((th, d), lambda i, j: (j, 0)),
        ],
        out_specs=pl.BlockSpec((tm, d), lambda i, j: (i, 0)),
        out_shape=jax.ShapeDtypeStruct((m, d), F32),
        scratch_shapes=[pltpu.VMEM((tm, d), BF16)],
        compiler_params=_params("parallel", "arbitrary"),
        name="ffn",
    )(h, g.reshape(1, d), wg, wu, wd)


def _ple_kernel(x_ref, g_ref, wg_ref, p_ref, wp_ref, fg_ref, o_ref, *, final_norm):
    x = x_ref[...]
    gate = jax.nn.sigmoid(_dot(_rms(x, g_ref[...]).astype(BF16), wg_ref[...]))
    out = x + _dot(p_ref[...].astype(BF16), wp_ref[...]) * gate
    o_ref[...] = _rms(out, fg_ref[...]) if final_norm else out


def _ple(h, g, wg, p, wp, fg, tm, final_norm):
    m, d = h.shape
    pd = p.shape[1]
    return pl.pallas_call(
        functools.partial(_ple_kernel, final_norm=final_norm),
        grid=(m // tm,),
        in_specs=[
            pl.BlockSpec((tm, d), lambda i: (i, 0)),
            _resident((1, d)),
            _resident(wg.shape),
            pl.BlockSpec((tm, pd), lambda i: (i, 0)),
            _resident(wp.shape),
            _resident((1, d)),
        ],
        out_specs=pl.BlockSpec((tm, d), lambda i: (i, 0)),
        out_shape=jax.ShapeDtypeStruct((m, d), F32),
        compiler_params=_params("parallel"),
        name="ple",
    )(h, g.reshape(1, d), wg, p, wp, fg.reshape(1, d))


def _key_to_float(key):
    bits = key ^ ((key >> 31) & 0x7FFFFFFF)
    return lax.bitcast_convert_type(bits, F32)


def _dsa_kernel(qi_ref, qsm_ref, ksm_ref, qa_ref, c_ref, o_ref,
                sc_ref, ct_ref, acc_ref, m_ref, l_ref, *, topk, qb_size, seq):
    t = qb_size
    qb = pl.program_id(1)
    nk = qb + 1

    @pl.when(qb == 0)
    def _():
        ct_ref[...] = jnp.transpose(c_ref[...].astype(F32)).astype(BF16)

    row = lax.broadcasted_iota(I32, (t, t), 0)
    col = lax.broadcasted_iota(I32, (t, t), 1)
    limit = (((qb * t + col) >> CHUNK_SHIFT) + 1) << CHUNK_SHIFT

    w_t = jnp.transpose(qsm_ref[...].astype(F32))[SM_W:SM_W + IDX_HEADS, :]

    def score_tile(kt, carry):
        rows = pl.ds(pl.multiple_of(kt * t, t), t)
        kk = ksm_ref[rows, :][:, :IDX_DIM]
        sc = jnp.zeros((t, t), F32)
        for h in range(IDX_HEADS):
            qh = qi_ref[:, h * IDX_DIM:(h + 1) * IDX_DIM]
            sc = sc + jnp.maximum(_dot_nt(kk, qh), 0.0) * w_t[h:h + 1, :]
        sc = sc * (IDX_DIM ** -0.5 * IDX_HEADS ** -0.5)
        sc_ref[rows, :] = jnp.where(kt * t + row < limit, sc, -jnp.inf)
        return carry

    lax.fori_loop(0, nk, score_tile, 0)

    def count(pred):
        def body(kt, acc):
            rows = pl.ds(pl.multiple_of(kt * t, t), t)
            hit = pred(sc_ref[rows, :]).astype(I32)
            return acc + jnp.sum(hit.reshape(t // SUBLANES, SUBLANES, t), axis=0)
        acc = lax.fori_loop(0, nk, body, jnp.zeros((SUBLANES, t), I32))
        return jnp.sum(acc, axis=0, keepdims=True)

    def count_ge(key):
        cand = _key_to_float(key)
        return count(lambda s: s >= cand)

    zero_key = jnp.zeros((1, t), I32)
    res0 = jnp.where(count_ge(zero_key) >= topk, 0, INT_MIN).astype(I32)

    def bisect(i, res):
        cand = res | jnp.left_shift(jnp.int32(1), 30 - i)
        return jnp.where(count_ge(cand) >= topk, cand, res)

    res = lax.fori_loop(0, 31, bisect, res0)
    thr = _key_to_float(jnp.maximum(res, KEY_NEG_INF))
    n_ge = count(lambda s: s >= thr) + jnp.where(thr == -jnp.inf, seq - nk * t, 0)

    @pl.when(jnp.max(n_ge) > topk)
    def _():
        need = (topk - count(lambda s: s > thr)).astype(F32)
        tri = (row >= col).astype(BF16)

        def body(kt, seen):
            rows = pl.ds(pl.multiple_of(kt * t, t), t)
            s = sc_ref[rows, :]
            eq = s == thr
            rank = _dot(tri, eq.astype(BF16)) + seen
            drop = (eq & (rank > need)) | (kt * t + row >= limit)
            sc_ref[rows, :] = jnp.where(drop, jnp.nan, s)
            return seen + jnp.sum(eq.astype(F32), axis=0, keepdims=True)

        lax.fori_loop(0, nk, body, jnp.zeros((1, t), F32))

    m_ref[...] = jnp.full(m_ref.shape, -1e30, F32)
    l_ref[...] = jnp.zeros(l_ref.shape, F32)
    acc_ref[...] = jnp.zeros(acc_ref.shape, F32)

    def attend(kt, carry):
        rows = pl.ds(pl.multiple_of(kt * t, t), t)
        valid = sc_ref[rows, :] >= thr
        ck = c_ref[rows, :]
        ckt = ct_ref[:, rows]
        c2 = (A_LATENT ** -0.5) * LOG2_E
        for g in range(0, A_HEADS, DSA_HEAD_GROUP):
            group = range(g, g + DSA_HEAD_GROUP)
            raws = [jnp.where(valid, _dot_nt(ck, qa_ref[:, h * A_LATENT:(h + 1) * A_LATENT]), -1e30)
                    for h in group]
            ps, alphas = [], []
            for h, raw in zip(group, raws):
                m_old = m_ref[h:h + 1, :]
                m_new = jnp.maximum(m_old, jnp.max(raw, axis=0, keepdims=True))
                alpha = jnp.exp2((m_old - m_new) * c2)
                p = jnp.exp2((raw - m_new) * c2)
                l_ref[h:h + 1, :] = alpha * l_ref[h:h + 1, :] + jnp.sum(p, axis=0, keepdims=True)
                m_ref[h:h + 1, :] = m_new
                ps.append(p.astype(BF16))
                alphas.append(alpha)
            pvs = [_dot(ckt, p) for p in ps]
            for h, alpha, pv in zip(group, alphas, pvs):
                hs = slice(h * A_LATENT, (h + 1) * A_LATENT)
                acc_ref[hs, :] = alpha * acc_ref[hs, :] + pv
        return carry

    lax.fori_loop(0, nk, attend, 0)

    for h in range(A_HEADS):
        hs = slice(h * A_LATENT, (h + 1) * A_LATENT)
        out = acc_ref[hs, :] / l_ref[h:h + 1, :]
        o_ref[:, hs] = jnp.transpose(out).astype(o_ref.dtype)


def _dsa(u, batch, seq, qb_size):
    m = batch * seq
    nqb = seq // qb_size
    topk = min(A_TOPK_MAX, seq // 4)
    wide = A_HEADS * A_LATENT
    return pl.pallas_call(
        functools.partial(_dsa_kernel, topk=topk, qb_size=qb_size, seq=seq),
        grid=(batch, nqb),
        in_specs=[
            pl.BlockSpec((qb_size, wide), lambda b, q: (b * nqb + q, U_QI // wide)),
            pl.BlockSpec((qb_size, LANES), lambda b, q: (b * nqb + q, U_SMALL // LANES)),
            pl.BlockSpec((seq, LANES), lambda b, q: (b, U_SMALL // LANES)),
            pl.BlockSpec((qb_size, wide), lambda b, q: (b * nqb + q, U_QA // wide)),
            pl.BlockSpec((seq, LANES), lambda b, q: (b, U_CA // LANES)),
        ],
        out_specs=pl.BlockSpec((qb_size, wide), lambda b, q: (b * nqb + q, 0)),
        out_shape=jax.ShapeDtypeStruct((m, wide), BF16),
        scratch_shapes=[
            pltpu.VMEM((seq, qb_size), F32),
            pltpu.VMEM((A_LATENT, seq), BF16),
            pltpu.VMEM((wide, qb_size), F32),
            pltpu.VMEM((A_HEADS, qb_size), F32),
            pltpu.VMEM((A_HEADS, qb_size), F32),
        ],
        compiler_params=_params("parallel", "arbitrary"),
        name="dsa",
    )(u, u, u, u, u)


def _gla_kernel(q_ref, k_ref, v_ref, g_ref, sm_ref, wgk_ref, bgk_ref, gn_ref, o_ref, st_ref,
                *, nchunks):
    c = CHUNK
    row = lax.broadcasted_iota(I32, (c, c), 0)
    col = lax.broadcasted_iota(I32, (c, c), 1)
    causal = row >= col
    tri = causal.astype(BF16)
    st_ref[...] = jnp.zeros(st_ref.shape, F32)
    wgk = wgk_ref[...]
    bgk = bgk_ref[...]
    gn = gn_ref[...]

    def chunk(ci, carry):
        rows = pl.ds(pl.multiple_of(ci * c, c), c)
        x = _dot(sm_ref[rows, :], wgk) + bgk
        gk = -(jnp.maximum(-x, 0.0) + _softplus_tail(x)) / B_GATE_NORM
        hi, lo = _split_bf16(gk)
        cum_all = _dot(tri, hi) + _dot(tri, lo)
        heads = range(B_HEADS)
        ksl = [slice(h * B_DK, (h + 1) * B_DK) for h in heads]
        vsl = [slice(h * B_DV, (h + 1) * B_DV) for h in heads]
        q_es, k_es, k_ds, decays = [], [], [], []
        for ks in ksl:
            cum = cum_all[:, ks]
            last = cum[c - 1:c, :]
            q = q_ref[rows, ks].astype(F32) * (B_DK ** -0.5)
            k = k_ref[rows, ks].astype(F32)
            q_es.append((q * jnp.exp(cum)).astype(BF16))
            k_es.append((k * jnp.exp(-cum)).astype(BF16))
            k_ds.append((k * jnp.exp(last - cum)).astype(BF16))
            decays.append(jnp.exp(last))
        vs_ = [v_ref[rows, vs] for vs in vsl]
        sts = [st_ref[h] for h in heads]
        atts = [_dot_nt(q_e, k_e) for q_e, k_e in zip(q_es, k_es)]
        inter = [_dot_nt(q_e, st.astype(BF16)) for q_e, st in zip(q_es, sts)]
        kvs = [lax.dot_general(v, k_d, TN_DIMS, preferred_element_type=F32)
               for v, k_d in zip(vs_, k_ds)]
        for h in heads:
            st_ref[h] = sts[h] * decays[h] + kvs[h]
        atts = [jnp.where(causal, a, 0.0).astype(BF16) for a in atts]
        intra = [_dot(a, v) for a, v in zip(atts, vs_)]
        for h in heads:
            gate = g_ref[rows, vsl[h]].astype(F32)
            o = intra[h] + inter[h]
            o_ref[rows, vsl[h]] = (_rms(o, gn) * (gate * jax.nn.sigmoid(gate))).astype(o_ref.dtype)
        return carry

    lax.fori_loop(0, nchunks, chunk, 0)


def _gla(u, wgk_pad, bgk, gnorm, batch, seq):
    m = batch * seq
    qk_wide = B_HEADS * B_DK
    v_wide = B_HEADS * B_DV
    return pl.pallas_call(
        functools.partial(_gla_kernel, nchunks=seq // CHUNK),
        grid=(batch,),
        in_specs=[
            pl.BlockSpec((seq, qk_wide), lambda b: (b, U_QB // qk_wide)),
            pl.BlockSpec((seq, qk_wide), lambda b: (b, U_KB // qk_wide)),
            pl.BlockSpec((seq, v_wide), lambda b: (b, U_VB // v_wide)),
            pl.BlockSpec((seq, v_wide), lambda b: (b, U_GB // v_wide)),
            pl.BlockSpec((seq, LANES), lambda b: (b, U_SMALL // LANES)),
            pl.BlockSpec((LANES, qk_wide), lambda b: (0, 0)),
            pl.BlockSpec((1, qk_wide), lambda b: (0, 0)),
            pl.BlockSpec((1, B_DV), lambda b: (0, 0)),
        ],
        out_specs=pl.BlockSpec((seq, v_wide), lambda b: (b, 0)),
        out_shape=jax.ShapeDtypeStruct((m, v_wide), BF16),
        scratch_shapes=[pltpu.VMEM((B_HEADS, B_DV, B_DK), F32)],
        compiler_params=_params("parallel"),
        name="gla",
    )(u, u, u, u, u, wgk_pad, bgk.reshape(1, -1), gnorm.reshape(1, -1))


def _sb_kernel(q_ref, k_ref, v_ref, o_ref, *, blk, nblk, head_dim, heads):
    t = blk
    d = head_dim
    row = lax.broadcasted_iota(I32, (t, t), 0)
    col = lax.broadcasted_iota(I32, (t, t), 1)
    upper = (row > col).astype(BF16)
    upper2 = jnp.concatenate([upper, upper], axis=0)
    diag = col < row
    scale = d ** -0.5

    hsl = [slice(h * d, (h + 1) * d) for h in range(heads)]

    def block(qrows, krows, runs, mask):
        zs = [_dot_nt(q_ref[qrows, hs], k_ref[krows, hs]) * scale for hs in hsl]
        lms, log_bs = [], []
        for z in zs:
            log_1mb = -(jnp.maximum(z, 0.0) + jnp.log(1.0 + jnp.exp(-jnp.abs(z))))
            log_bs.append(log_1mb + z)
            lms.append(log_1mb if mask is None else jnp.where(mask, log_1mb, 0.0))
        splits = [jnp.concatenate(_split_bf16(lm), axis=1) for lm in lms]
        sufs = [_dot(sp, upper2) for sp in splits]
        ws = []
        for suf, log_b, run in zip(sufs, log_bs, runs):
            w = jnp.exp(suf + run + log_b)
            ws.append((w if mask is None else jnp.where(mask, w, 0.0)).astype(BF16))
        outs = [_dot(w, v_ref[krows, hs]) for w, hs in zip(ws, hsl)]
        new_runs = [run + jnp.sum(lm, axis=1, keepdims=True) for run, lm in zip(runs, lms)]
        return new_runs, outs

    def q_block(qb, carry):
        qrows = pl.ds(pl.multiple_of(qb * t, t), t)
        runs, accs = block(qrows, qrows, [jnp.zeros((t, 1), F32)] * heads, diag)

        def alive(rs):
            top = jnp.max(rs[0])
            for r in rs[1:]:
                top = jnp.maximum(top, jnp.max(r))
            return (top >= SB_EXP_UNDERFLOW).astype(I32)

        def cond(state):
            i, live, _, _ = state
            return jnp.logical_and(i <= qb, live > 0)

        def k_block(state):
            i, _, rs, acs = state
            krows = pl.ds(pl.multiple_of((qb - i) * t, t), t)
            new_rs, outs = block(qrows, krows, list(rs), None)
            new_acs = [a + o for a, o in zip(acs, outs)]
            return i + 1, alive(new_rs), tuple(new_rs), tuple(new_acs)

        state = (jnp.int32(1), alive(runs), tuple(runs), tuple(accs))
        _, _, _, accs = lax.while_loop(cond, k_block, state)
        for h in range(heads):
            o_ref[qrows, h * d:(h + 1) * d] = accs[h].astype(o_ref.dtype)
        return carry

    lax.fori_loop(0, nblk, q_block, 0)


def _stick_breaking(qkv, batch, seq, blk):
    m = batch * seq
    d = qkv.shape[1] // (3 * C_HEADS)
    groups = C_HEADS // SB_HEADS_PER_STEP
    wide = SB_HEADS_PER_STEP * d
    return pl.pallas_call(
        functools.partial(_sb_kernel, blk=blk, nblk=seq // blk, head_dim=d, heads=SB_HEADS_PER_STEP),
        grid=(batch, groups),
        in_specs=[
            pl.BlockSpec((seq, wide), lambda b, g: (b, g)),
            pl.BlockSpec((seq, wide), lambda b, g: (b, groups + g)),
            pl.BlockSpec((seq, wide), lambda b, g: (b, 2 * groups + g)),
        ],
        out_specs=pl.BlockSpec((seq, wide), lambda b, g: (b, g)),
        out_shape=jax.ShapeDtypeStruct((m, C_HEADS * d), BF16),
        compiler_params=_params("parallel", "arbitrary"),
        name="stick_breaking",
    )(qkv, qkv, qkv)


def _permute_even_w_in(w):
    d = w.shape[0]
    qa, ca, qi, ki, wi, qb, kb, vb, gb, gk = jnp.split(
        w, [1024, 1152, 2176, 2240, 2256, 2768, 3280, 4304, 5328], axis=1)
    pad = jnp.zeros((d, LANES - IDX_DIM - IDX_HEADS - B_GATE_RANK), w.dtype)
    return jnp.concatenate([qa, qi, vb, gb, qb, kb, ca, ki, wi, gk, pad], axis=1)


def kernel(x, p, even_norm, even_w_in, even_w_gk, even_b_gk, even_gla_norm, even_w_out,
           odd_norm, odd_w_in, odd_w_out, ffn_norm, ffn_w_gate, ffn_w_up, ffn_w_down,
           ple_norm, ple_w_gate, ple_w_proj, final_norm):
    batch, seq, d = x.shape
    depth = p.shape[0]
    m = batch * seq
    h = x.reshape(m, d)
    tm = _tile(m, 1024)
    tm_ffn = _tile(m, 512)
    blk = _tile(seq, 256)

    for i in range(depth):
        j = i // 2
        if i % 2 == 0:
            w_in = _permute_even_w_in(even_w_in[j]).astype(BF16)
            u = _norm_matmul(h, even_norm[j], w_in, tm, _tile(U_WIDTH, 768))
            o_a = _dsa(u, batch, seq, blk)
            wgk_pad = jnp.zeros((LANES, B_HEADS * B_DK), F32)
            wgk_pad = wgk_pad.at[SM_GK:SM_GK + B_GATE_RANK].set(even_w_gk[j]).astype(BF16)
            o_b = _gla(u, wgk_pad, even_b_gk[j], even_gla_norm[j], batch, seq)
            w_out = even_w_out[j].astype(BF16)
            split = A_HEADS * A_LATENT
            h = _proj_residual(h, [o_a, o_b], [w_out[:split], w_out[split:]], tm_ffn)
        else:
            qkv = _norm_matmul(h, odd_norm[j], odd_w_in[j].astype(BF16), tm, _tile(3 * d, 768))
            o = _stick_breaking(qkv, batch, seq, blk)
            h = _proj_residual(h, [o], [odd_w_out[j].astype(BF16)], tm_ffn)
        h = _ffn(h, ffn_norm[i], ffn_w_gate[i].astype(BF16), ffn_w_up[i].astype(BF16),
                 ffn_w_down[i].astype(BF16), tm_ffn, _tile(ffn_w_gate.shape[2], 512))
        h = _ple(h, ple_norm[i], ple_w_gate[i].astype(BF16), p[i].reshape(m, -1),
                 ple_w_proj[i].astype(BF16), final_norm, tm_ffn, final_norm=(i == depth - 1))
    return h.reshape(batch, seq, d)
```

```python
import functools

import jax
import jax.numpy as jnp
from jax import lax
from jax.experimental import pallas as pl
from jax.experimental.pallas import tpu as pltpu

F32 = jnp.float32
BF16 = jnp.bfloat16
I32 = jnp.int32

EPS = 1e-6
CHUNK = 64
CHUNK_SHIFT = CHUNK.bit_length() - 1
A_HEADS = 8
A_LATENT = 128
A_TOPK_MAX = 256
IDX_HEADS = 16
IDX_DIM = 64
B_HEADS = 4
B_DK = 128
B_DV = 256
B_GATE_RANK = 16
B_GATE_NORM = 16.0
GLA_CHUNKS_PER_STEP = 4
C_HEADS = 16
DSA_HEAD_GROUP = 8
LOG2_E = 1.4426950408889634
SB_HEADS_PER_STEP = 4
SB_EXP_UNDERFLOW = -104.0

LANES = 128
SUBLANES = 8
VMEM_LIMIT_BYTES = 56 * 1024 * 1024

U_QA = 0
U_QI = 1024
U_VB = 2048
U_GB = 3072
U_QB = 4096
U_KB = 4608
U_CA = 5120
U_SMALL = 5248
U_WIDTH = 5376
SM_W = 64
SM_GK = 80

NT_DIMS = (((1,), (1,)), ((), ()))
TN_DIMS = (((0,), (0,)), ((), ()))

KEY_NEG_INF = -2139095041
INT_MIN = -2147483648


def _params(*sem):
    return pltpu.CompilerParams(dimension_semantics=sem, vmem_limit_bytes=VMEM_LIMIT_BYTES)


def _tile(n, pref):
    if n <= pref:
        return n
    t = (pref // LANES) * LANES
    while n % t:
        t -= LANES
    return t


def _rms(x, g):
    ms = jnp.mean(x * x, axis=-1, keepdims=True)
    return x * lax.rsqrt(ms + EPS) * g


def _dot(a, b):
    return jnp.dot(a, b, preferred_element_type=F32)


def _dot_nt(a, b):
    return lax.dot_general(a, b, NT_DIMS, preferred_element_type=F32)


def _softplus_tail(z):
    return jnp.log1p(jnp.exp(-jnp.abs(z)))


def _split_bf16(x):
    hi = x.astype(BF16)
    lo = (x - hi.astype(F32)).astype(BF16)
    return hi, lo


def _norm_matmul_kernel(x_ref, g_ref, w_ref, o_ref, xn_ref):
    @pl.when(pl.program_id(1) == 0)
    def _():
        xn_ref[...] = _rms(x_ref[...], g_ref[...]).astype(BF16)

    o_ref[...] = _dot(xn_ref[...], w_ref[...]).astype(o_ref.dtype)


def _norm_matmul(x, g, w, tm, tn):
    m, k = x.shape
    n = w.shape[1]
    return pl.pallas_call(
        _norm_matmul_kernel,
        grid=(m // tm, n // tn),
        in_specs=[
            pl.BlockSpec((tm, k), lambda i, j: (i, 0)),
            pl.BlockSpec((1, k), lambda i, j: (0, 0)),
            pl.BlockSpec((k, tn), lambda i, j: (0, j)),
        ],
        out_specs=pl.BlockSpec((tm, tn), lambda i, j: (i, j)),
        out_shape=jax.ShapeDtypeStruct((m, n), BF16),
        scratch_shapes=[pltpu.VMEM((tm, k), BF16)],
        compiler_params=_params("parallel", "arbitrary"),
        name="norm_matmul",
    )(x, g.reshape(1, k), w)


def _proj_residual_kernel(*refs, n_in):
    h_ref, g_ref = refs[0], refs[1]
    a_refs = refs[2:2 + n_in]
    w_refs = refs[2 + n_in:2 + 2 * n_in]
    o_ref, xn_ref = refs[2 + 2 * n_in:]
    acc = h_ref[...]
    for a_ref, w_ref in zip(a_refs, w_refs):
        acc = acc + _dot(a_ref[...], w_ref[...])
    o_ref[...] = acc
    xn_ref[...] = _rms(acc, g_ref[...]).astype(BF16)


def _resident(shape):
    return pl.BlockSpec(shape, lambda i: (0,) * len(shape), pipeline_mode=pl.Buffered(1))


def _proj_residual(h, g, a_list, w_list, tm):
    m, n = h.shape
    n_in = len(a_list)
    in_specs = [pl.BlockSpec((tm, n), lambda i: (i, 0)), _resident((1, n))]
    in_specs += [pl.BlockSpec((tm, a.shape[1]), lambda i: (i, 0)) for a in a_list]
    in_specs += [_resident(w.shape) for w in w_list]
    row_block = pl.BlockSpec((tm, n), lambda i: (i, 0))
    return pl.pallas_call(
        functools.partial(_proj_residual_kernel, n_in=n_in),
        grid=(m // tm,),
        in_specs=in_specs,
        out_specs=[row_block, row_block],
        out_shape=[jax.ShapeDtypeStruct((m, n), F32), jax.ShapeDtypeStruct((m, n), BF16)],
        compiler_params=_params("parallel"),
        name="proj_residual",
    )(h, g.reshape(1, n), *a_list, *w_list)


def _ffn_kernel(xn_ref, wg_ref, wu_ref, wd_ref, o_ref):
    @pl.when(pl.program_id(1) == 0)
    def _():
        o_ref[...] = jnp.zeros(o_ref.shape, F32)

    xn = xn_ref[...]
    gate = _dot(xn, wg_ref[...])
    up = _dot(xn, wu_ref[...])
    act = (gate * jax.nn.sigmoid(gate) * up).astype(BF16)
    o_ref[...] += _dot(act, wd_ref[...])


def _ffn(xn, wg, wu, wd, tm, th):
    m, d = xn.shape
    hid = wg.shape[1]
    return pl.pallas_call(
        _ffn_kernel,
        grid=(m // tm, hid // th),
        in_specs=[
            pl.BlockSpec((tm, d), lambda i, j: (i, 0)),
            pl.BlockSpec((d, th), lambda i, j: (0, j)),
            pl.BlockSpec((d, th), lambda i, j: (0, j)),
            pl.BlockSpec((th, d), lambda i, j: (j, 0)),
        ],
        out_specs=pl.BlockSpec((tm, d), lambda i, j: (i, 0)),
        out_shape=jax.ShapeDtypeStruct((m, d), F32),
        compiler_params=_params("parallel", "arbitrary"),
        name="ffn",
    )(xn, wg, wu, wd)


def _ple_kernel(x_ref, y_ref, g_ref, wg_ref, p_ref, wp_ref, fg_ref, o_ref, *, final_norm):
    x = x_ref[...] + y_ref[...]
    gate = jax.nn.sigmoid(_dot(_rms(x, g_ref[...]).astype(BF16), wg_ref[...]))
    out = x + _dot(p_ref[...].astype(BF16), wp_ref[...]) * gate
    o_ref[...] = _rms(out, fg_ref[...]) if final_norm else out


def _ple(h, y, g, wg, p, wp, fg, tm, final_norm):
    m, d = h.shape
    pd = p.shape[1]
    row_block = pl.BlockSpec((tm, d), lambda i: (i, 0))
    return pl.pallas_call(
        functools.partial(_ple_kernel, final_norm=final_norm),
        grid=(m // tm,),
        in_specs=[
            row_block,
            row_block,
            _resident((1, d)),
            _resident(wg.shape),
            pl.BlockSpec((tm, pd), lambda i: (i, 0)),
            _resident(wp.shape),
            _resident((1, d)),
        ],
        out_specs=row_block,
        out_shape=jax.ShapeDtypeStruct((m, d), F32),
        compiler_params=_params("parallel"),
        name="ple",
    )(h, y, g.reshape(1, d), wg, p, wp, fg.reshape(1, d))


def _key_to_float(key):
    bits = key ^ ((key >> 31) & 0x7FFFFFFF)
    return lax.bitcast_convert_type(bits, F32)


def _dsa_kernel(qi_ref, qsm_ref, ksm_ref, qa_ref, c_ref, o_ref,
                sc_ref, ct_ref, acc_ref, m_ref, l_ref, *, topk, qb_size, seq):
    t = qb_size
    qb = pl.program_id(1)
    nk = qb + 1

    @pl.when(qb == 0)
    def _():
        ct_ref[...] = jnp.transpose(c_ref[...].astype(F32)).astype(BF16)

    row = lax.broadcasted_iota(I32, (t, t), 0)
    col = lax.broadcasted_iota(I32, (t, t), 1)
    limit = (((qb * t + col) >> CHUNK_SHIFT) + 1) << CHUNK_SHIFT

    w_t = jnp.transpose(qsm_ref[...].astype(F32))[SM_W:SM_W + IDX_HEADS, :]

    def score_tile(kt, carry):
        rows = pl.ds(pl.multiple_of(kt * t, t), t)
        kk = ksm_ref[rows, :][:, :IDX_DIM]
        sc = jnp.zeros((t, t), F32)
        for h in range(IDX_HEADS):
            qh = qi_ref[:, h * IDX_DIM:(h + 1) * IDX_DIM]
            sc = sc + jnp.maximum(_dot_nt(kk, qh), 0.0) * w_t[h:h + 1, :]
        sc = sc * (IDX_DIM ** -0.5 * IDX_HEADS ** -0.5)
        sc_ref[rows, :] = jnp.where(kt * t + row < limit, sc, -jnp.inf)
        return carry

    lax.fori_loop(0, nk, score_tile, 0)

    def count(pred):
        def body(kt, acc):
            rows = pl.ds(pl.multiple_of(kt * t, t), t)
            hit = pred(sc_ref[rows, :]).astype(I32)
            return acc + jnp.sum(hit.reshape(t // SUBLANES, SUBLANES, t), axis=0)
        acc = lax.fori_loop(0, nk, body, jnp.zeros((SUBLANES, t), I32))
        return jnp.sum(acc, axis=0, keepdims=True)

    def count_ge(key):
        cand = _key_to_float(key)
        return count(lambda s: s >= cand)

    zero_key = jnp.zeros((1, t), I32)
    res0 = jnp.where(count_ge(zero_key) >= topk, 0, INT_MIN).astype(I32)

    def bisect(i, res):
        cand = res | jnp.left_shift(jnp.int32(1), 30 - i)
        return jnp.where(count_ge(cand) >= topk, cand, res)

    res = lax.fori_loop(0, 31, bisect, res0)
    thr = _key_to_float(jnp.maximum(res, KEY_NEG_INF))
    n_ge = count(lambda s: s >= thr) + jnp.where(thr == -jnp.inf, seq - nk * t, 0)

    @pl.when(jnp.max(n_ge) > topk)
    def _():
        need = (topk - count(lambda s: s > thr)).astype(F32)
        tri = (row >= col).astype(BF16)

        def body(kt, seen):
            rows = pl.ds(pl.multiple_of(kt * t, t), t)
            s = sc_ref[rows, :]
            eq = s == thr
            rank = _dot(tri, eq.astype(BF16)) + seen
            drop = (eq & (rank > need)) | (kt * t + row >= limit)
            sc_ref[rows, :] = jnp.where(drop, jnp.nan, s)
            return seen + jnp.sum(eq.astype(F32), axis=0, keepdims=True)

        lax.fori_loop(0, nk, body, jnp.zeros((1, t), F32))

    m_ref[...] = jnp.full(m_ref.shape, -1e30, F32)
    l_ref[...] = jnp.zeros(l_ref.shape, F32)
    acc_ref[...] = jnp.zeros(acc_ref.shape, F32)

    def attend(kt, carry):
        rows = pl.ds(pl.multiple_of(kt * t, t), t)
        valid = sc_ref[rows, :] >= thr
        ck = c_ref[rows, :]
        ckt = ct_ref[:, rows]
        c2 = (A_LATENT ** -0.5) * LOG2_E
        for g in range(0, A_HEADS, DSA_HEAD_GROUP):
            group = range(g, g + DSA_HEAD_GROUP)
            raws = [jnp.where(valid, _dot_nt(ck, qa_ref[:, h * A_LATENT:(h + 1) * A_LATENT]), -1e30)
                    for h in group]
            ps, alphas = [], []
            for h, raw in zip(group, raws):
                m_old = m_ref[h:h + 1, :]
                m_new = jnp.maximum(m_old, jnp.max(raw, axis=0, keepdims=True))
                alpha = jnp.exp2((m_old - m_new) * c2)
                p = jnp.exp2((raw - m_new) * c2)
                l_ref[h:h + 1, :] = alpha * l_ref[h:h + 1, :] + jnp.sum(p, axis=0, keepdims=True)
                m_ref[h:h + 1, :] = m_new
                ps.append(p.astype(BF16))
                alphas.append(alpha)
            pvs = [_dot(ckt, p) for p in ps]
            for h, alpha, pv in zip(group, alphas, pvs):
                hs = slice(h * A_LATENT, (h + 1) * A_LATENT)
                acc_ref[hs, :] = alpha * acc_ref[hs, :] + pv
        return carry

    lax.fori_loop(0, nk, attend, 0)

    for h in range(A_HEADS):
        hs = slice(h * A_LATENT, (h + 1) * A_LATENT)
        out = acc_ref[hs, :] / l_ref[h:h + 1, :]
        o_ref[:, hs] = jnp.transpose(out).astype(o_ref.dtype)


def _dsa(u, batch, seq, qb_size):
    m = batch * seq
    nqb = seq // qb_size
    topk = min(A_TOPK_MAX, seq // 4)
    wide = A_HEADS * A_LATENT
    return pl.pallas_call(
        functools.partial(_dsa_kernel, topk=topk, qb_size=qb_size, seq=seq),
        grid=(batch, nqb),
        in_specs=[
            pl.BlockSpec((qb_size, wide), lambda b, q: (b * nqb + q, U_QI // wide)),
            pl.BlockSpec((qb_size, LANES), lambda b, q: (b * nqb + q, U_SMALL // LANES)),
            pl.BlockSpec((seq, LANES), lambda b, q: (b, U_SMALL // LANES)),
            pl.BlockSpec((qb_size, wide), lambda b, q: (b * nqb + q, U_QA // wide)),
            pl.BlockSpec((seq, LANES), lambda b, q: (b, U_CA // LANES)),
        ],
        out_specs=pl.BlockSpec((qb_size, wide), lambda b, q: (b * nqb + q, 0)),
        out_shape=jax.ShapeDtypeStruct((m, wide), BF16),
        scratch_shapes=[
            pltpu.VMEM((seq, qb_size), F32),
            pltpu.VMEM((A_LATENT, seq), BF16),
            pltpu.VMEM((wide, qb_size), F32),
            pltpu.VMEM((A_HEADS, qb_size), F32),
            pltpu.VMEM((A_HEADS, qb_size), F32),
        ],
        compiler_params=_params("parallel", "arbitrary"),
        name="dsa",
    )(u, u, u, u, u)


def _gla_kernel(q_ref, k_ref, v_ref, g_ref, sm_ref, wgk_ref, bgk_ref, gn_ref, o_ref, st_ref,
                *, nchunks):
    c = CHUNK
    n = GLA_CHUNKS_PER_STEP
    span = n * c
    row = lax.broadcasted_iota(I32, (span, span), 0)
    col = lax.broadcasted_iota(I32, (span, span), 1)
    tri = ((row >= col) & ((row >> CHUNK_SHIFT) == (col >> CHUNK_SHIFT))).astype(BF16)
    causal = lax.broadcasted_iota(I32, (c, c), 0) >= lax.broadcasted_iota(I32, (c, c), 1)
    st_ref[...] = jnp.zeros(st_ref.shape, F32)
    wgk = wgk_ref[...]
    bgk = bgk_ref[...]
    gn = gn_ref[...]
    heads = range(B_HEADS)
    ksl = [slice(h * B_DK, (h + 1) * B_DK) for h in heads]
    vsl = [slice(h * B_DV, (h + 1) * B_DV) for h in heads]
    work = [(s, h) for s in range(n) for h in heads]

    def step(si, carry):
        base = pl.multiple_of(si * span, span)
        rows = pl.ds(base, span)
        x = _dot(sm_ref[rows, :], wgk) + bgk
        gk = -(jnp.maximum(-x, 0.0) + _softplus_tail(x)) / B_GATE_NORM
        hi, lo = _split_bf16(gk)
        cum_all = _dot(tri, hi) + _dot(tri, lo)
        q_es, k_es, k_ds, decays, vs_ = {}, {}, {}, {}, {}
        for s, h in work:
            sub = pl.ds(base + s * c, c)
            cum = cum_all[s * c:(s + 1) * c, ksl[h]]
            last = cum[c - 1:c, :]
            q = q_ref[sub, ksl[h]].astype(F32) * (B_DK ** -0.5)
            k = k_ref[sub, ksl[h]].astype(F32)
            q_es[s, h] = (q * jnp.exp(cum)).astype(BF16)
            k_es[s, h] = (k * jnp.exp(-cum)).astype(BF16)
            k_ds[s, h] = (k * jnp.exp(last - cum)).astype(BF16)
            decays[s, h] = jnp.exp(last)
            vs_[s, h] = v_ref[sub, vsl[h]]
        atts = {w: _dot_nt(q_es[w], k_es[w]) for w in work}
        kvs = {w: lax.dot_general(vs_[w], k_ds[w], TN_DIMS, preferred_element_type=F32)
               for w in work}
        sts = {h: st_ref[h] for h in heads}
        inter = {}
        for s in range(n):
            for h in heads:
                inter[s, h] = _dot_nt(q_es[s, h], sts[h].astype(BF16))
            for h in heads:
                sts[h] = sts[h] * decays[s, h] + kvs[s, h]
        for h in heads:
            st_ref[h] = sts[h]
        atts = {w: jnp.where(causal, atts[w], 0.0).astype(BF16) for w in work}
        intra = {w: _dot(atts[w], vs_[w]) for w in work}
        for s, h in work:
            sub = pl.ds(base + s * c, c)
            gate = g_ref[sub, vsl[h]].astype(F32)
            o = intra[s, h] + inter[s, h]
            o_ref[sub, vsl[h]] = (_rms(o, gn) * (gate * jax.nn.sigmoid(gate))).astype(o_ref.dtype)
        return carry

    lax.fori_loop(0, nchunks // n, step, 0)


def _gla(u, wgk_pad, bgk, gnorm, batch, seq):
    m = batch * seq
    qk_wide = B_HEADS * B_DK
    v_wide = B_HEADS * B_DV
    return pl.pallas_call(
        functools.partial(_gla_kernel, nchunks=seq // CHUNK),
        grid=(batch,),
        in_specs=[
            pl.BlockSpec((seq, qk_wide), lambda b: (b, U_QB // qk_wide)),
            pl.BlockSpec((seq, qk_wide), lambda b: (b, U_KB // qk_wide)),
            pl.BlockSpec((seq, v_wide), lambda b: (b, U_VB // v_wide)),
            pl.BlockSpec((seq, v_wide), lambda b: (b, U_GB // v_wide)),
            pl.BlockSpec((seq, LANES), lambda b: (b, U_SMALL // LANES)),
            pl.BlockSpec((LANES, qk_wide), lambda b: (0, 0)),
            pl.BlockSpec((1, qk_wide), lambda b: (0, 0)),
            pl.BlockSpec((1, B_DV), lambda b: (0, 0)),
        ],
        out_specs=pl.BlockSpec((seq, v_wide), lambda b: (b, 0)),
        out_shape=jax.ShapeDtypeStruct((m, v_wide), BF16),
        scratch_shapes=[pltpu.VMEM((B_HEADS, B_DV, B_DK), F32)],
        compiler_params=_params("parallel"),
        name="gla",
    )(u, u, u, u, u, wgk_pad, bgk.reshape(1, -1), gnorm.reshape(1, -1))


def _sb_kernel(q_ref, k_ref, v_ref, o_ref, *, blk, nblk, head_dim, heads):
    t = blk
    d = head_dim
    row = lax.broadcasted_iota(I32, (t, t), 0)
    col = lax.broadcasted_iota(I32, (t, t), 1)
    upper = (row > col).astype(BF16)
    upper2 = jnp.concatenate([upper, upper], axis=0)
    diag = col < row
    scale = d ** -0.5

    hsl = [slice(h * d, (h + 1) * d) for h in range(heads)]

    def block(qrows, krows, runs, mask):
        zs = [_dot_nt(q_ref[qrows, hs], k_ref[krows, hs]) * scale for hs in hsl]
        lms, log_bs = [], []
        for z in zs:
            log_1mb = -(jnp.maximum(z, 0.0) + jnp.log(1.0 + jnp.exp(-jnp.abs(z))))
            log_bs.append(log_1mb + z)
            lms.append(log_1mb if mask is None else jnp.where(mask, log_1mb, 0.0))
        splits = [jnp.concatenate(_split_bf16(lm), axis=1) for lm in lms]
        sufs = [_dot(sp, upper2) for sp in splits]
        ws = []
        for suf, log_b, run in zip(sufs, log_bs, runs):
            w = jnp.exp(suf + run + log_b)
            ws.append((w if mask is None else jnp.where(mask, w, 0.0)).astype(BF16))
        outs = [_dot(w, v_ref[krows, hs]) for w, hs in zip(ws, hsl)]
        new_runs = [run + jnp.sum(lm, axis=1, keepdims=True) for run, lm in zip(runs, lms)]
        return new_runs, outs

    def q_block(qb, carry):
        qrows = pl.ds(pl.multiple_of(qb * t, t), t)
        runs, accs = block(qrows, qrows, [jnp.zeros((t, 1), F32)] * heads, diag)

        def alive(rs):
            top = jnp.max(rs[0])
            for r in rs[1:]:
                top = jnp.maximum(top, jnp.max(r))
            return (top >= SB_EXP_UNDERFLOW).astype(I32)

        def cond(state):
            i, live, _, _ = state
            return jnp.logical_and(i <= qb, live > 0)

        def k_block(state):
            i, _, rs, acs = state
            krows = pl.ds(pl.multiple_of((qb - i) * t, t), t)
            new_rs, outs = block(qrows, krows, list(rs), None)
            new_acs = [a + o for a, o in zip(acs, outs)]
            return i + 1, alive(new_rs), tuple(new_rs), tuple(new_acs)

        state = (jnp.int32(1), alive(runs), tuple(runs), tuple(accs))
        _, _, _, accs = lax.while_loop(cond, k_block, state)
        for h in range(heads):
            o_ref[qrows, h * d:(h + 1) * d] = accs[h].astype(o_ref.dtype)
        return carry

    lax.fori_loop(0, nblk, q_block, 0)


def _stick_breaking(qkv, batch, seq, blk):
    m = batch * seq
    d = qkv.shape[1] // (3 * C_HEADS)
    groups = C_HEADS // SB_HEADS_PER_STEP
    wide = SB_HEADS_PER_STEP * d
    return pl.pallas_call(
        functools.partial(_sb_kernel, blk=blk, nblk=seq // blk, head_dim=d, heads=SB_HEADS_PER_STEP),
        grid=(batch, groups),
        in_specs=[
            pl.BlockSpec((seq, wide), lambda b, g: (b, g)),
            pl.BlockSpec((seq, wide), lambda b, g: (b, groups + g)),
            pl.BlockSpec((seq, wide), lambda b, g: (b, 2 * groups + g)),
        ],
        out_specs=pl.BlockSpec((seq, wide), lambda b, g: (b, g)),
        out_shape=jax.ShapeDtypeStruct((m, C_HEADS * d), BF16),
        compiler_params=_params("parallel", "arbitrary"),
        name="stick_breaking",
    )(qkv, qkv, qkv)


def _permute_even_w_in(w):
    d = w.shape[0]
    qa, ca, qi, ki, wi, qb, kb, vb, gb, gk = jnp.split(
        w, [1024, 1152, 2176, 2240, 2256, 2768, 3280, 4304, 5328], axis=1)
    pad = jnp.zeros((d, LANES - IDX_DIM - IDX_HEADS - B_GATE_RANK), w.dtype)
    return jnp.concatenate([qa, qi, vb, gb, qb, kb, ca, ki, wi, gk, pad], axis=1)


def kernel(x, p, even_norm, even_w_in, even_w_gk, even_b_gk, even_gla_norm, even_w_out,
           odd_norm, odd_w_in, odd_w_out, ffn_norm, ffn_w_gate, ffn_w_up, ffn_w_down,
           ple_norm, ple_w_gate, ple_w_proj, final_norm):
    batch, seq, d = x.shape
    depth = p.shape[0]
    m = batch * seq
    h = x.reshape(m, d)
    tm = _tile(m, 1024)
    tm_row = _tile(m, 512)
    blk = _tile(seq, 256)

    for i in range(depth):
        j = i // 2
        if i % 2 == 0:
            w_in = _permute_even_w_in(even_w_in[j]).astype(BF16)
            u = _norm_matmul(h, even_norm[j], w_in, tm, _tile(U_WIDTH, 768))
            o_a = _dsa(u, batch, seq, blk)
            wgk_pad = jnp.zeros((LANES, B_HEADS * B_DK), F32)
            wgk_pad = wgk_pad.at[SM_GK:SM_GK + B_GATE_RANK].set(even_w_gk[j]).astype(BF16)
            o_b = _gla(u, wgk_pad, even_b_gk[j], even_gla_norm[j], batch, seq)
            w_out = even_w_out[j].astype(BF16)
            split = A_HEADS * A_LATENT
            h, xn = _proj_residual(h, ffn_norm[i], [o_a, o_b], [w_out[:split], w_out[split:]], tm_row)
        else:
            qkv = _norm_matmul(h, odd_norm[j], odd_w_in[j].astype(BF16), tm, _tile(3 * d, 768))
            o = _stick_breaking(qkv, batch, seq, blk)
            h, xn = _proj_residual(h, ffn_norm[i], [o], [odd_w_out[j].astype(BF16)], tm_row)
        y = _ffn(xn, ffn_w_gate[i].astype(BF16), ffn_w_up[i].astype(BF16),
                 ffn_w_down[i].astype(BF16), tm, _tile(ffn_w_gate.shape[2], 512))
        h = _ple(h, y, ple_norm[i], ple_w_gate[i].astype(BF16), p[i].reshape(m, -1),
                 ple_w_proj[i].astype(BF16), final_norm, tm_row, final_norm=(i == depth - 1))
    return h.reshape(batch, seq, d)
```

```python
import functools

import jax
import jax.numpy as jnp
from jax import lax
from jax.experimental import pallas as pl
from jax.experimental.pallas import tpu as pltpu

F32 = jnp.float32
BF16 = jnp.bfloat16
I32 = jnp.int32

EPS = 1e-6
CHUNK = 64
CHUNK_SHIFT = CHUNK.bit_length() - 1
A_HEADS = 8
A_LATENT = 128
A_TOPK_MAX = 256
IDX_HEADS = 16
IDX_DIM = 64
B_HEADS = 4
B_DK = 128
B_DV = 256
B_GATE_RANK = 16
B_GATE_NORM = 16.0
GLA_CHUNKS_PER_STEP = 4
C_HEADS = 16
DSA_HEAD_GROUP = 8
LOG2_E = 1.4426950408889634
SB_HEADS_PER_STEP = 4
SB_EXP_UNDERFLOW = -104.0

LANES = 128
SUBLANES = 8
VMEM_LIMIT_BYTES = 56 * 1024 * 1024

U_QA = 0
U_QI = 1024
U_VB = 2048
U_GB = 3072
U_QB = 4096
U_KB = 4608
U_CA = 5120
U_SMALL = 5248
U_WIDTH = 5376
SM_W = 64
SM_GK = 80

NT_DIMS = (((1,), (1,)), ((), ()))
TN_DIMS = (((0,), (0,)), ((), ()))

KEY_NEG_INF = -2139095041
INT_MIN = -2147483648


def _params(*sem):
    return pltpu.CompilerParams(dimension_semantics=sem, vmem_limit_bytes=VMEM_LIMIT_BYTES)


def _tile(n, pref):
    if n <= pref:
        return n
    t = (pref // LANES) * LANES
    while n % t:
        t -= LANES
    return t


def _rms(x, g):
    ms = jnp.mean(x * x, axis=-1, keepdims=True)
    return x * lax.rsqrt(ms + EPS) * g


def _dot(a, b):
    return jnp.dot(a, b, preferred_element_type=F32)


def _dot_nt(a, b):
    return lax.dot_general(a, b, NT_DIMS, preferred_element_type=F32)


def _softplus_tail(z):
    return jnp.log1p(jnp.exp(-jnp.abs(z)))


def _split_bf16(x):
    hi = x.astype(BF16)
    lo = (x - hi.astype(F32)).astype(BF16)
    return hi, lo


def _norm_matmul_kernel(x_ref, g_ref, w_ref, o_ref, xn_ref):
    @pl.when(pl.program_id(1) == 0)
    def _():
        xn_ref[...] = _rms(x_ref[...], g_ref[...]).astype(BF16)

    o_ref[...] = _dot(xn_ref[...], w_ref[...]).astype(o_ref.dtype)


def _norm_matmul(x, g, w, tm, tn):
    m, k = x.shape
    n = w.shape[1]
    return pl.pallas_call(
        _norm_matmul_kernel,
        grid=(m // tm, n // tn),
        in_specs=[
            pl.BlockSpec((tm, k), lambda i, j: (i, 0)),
            pl.BlockSpec((1, k), lambda i, j: (0, 0)),
            pl.BlockSpec((k, tn), lambda i, j: (0, j)),
        ],
        out_specs=pl.BlockSpec((tm, tn), lambda i, j: (i, j)),
        out_shape=jax.ShapeDtypeStruct((m, n), BF16),
        scratch_shapes=[pltpu.VMEM((tm, k), BF16)],
        compiler_params=_params("parallel", "arbitrary"),
        name="norm_matmul",
    )(x, g.reshape(1, k), w)


def _matmul_kernel(x_ref, w_ref, o_ref):
    o_ref[...] = _dot(x_ref[...], w_ref[...]).astype(o_ref.dtype)


def _matmul(xn, w, tm, tn):
    m, k = xn.shape
    n = w.shape[1]
    return pl.pallas_call(
        _matmul_kernel,
        grid=(m // tm, n // tn),
        in_specs=[
            pl.BlockSpec((tm, k), lambda i, j: (i, 0)),
            pl.BlockSpec((k, tn), lambda i, j: (0, j)),
        ],
        out_specs=pl.BlockSpec((tm, tn), lambda i, j: (i, j)),
        out_shape=jax.ShapeDtypeStruct((m, n), BF16),
        compiler_params=_params("parallel", "arbitrary"),
        name="matmul",
    )(xn, w)


def _proj_residual_kernel(*refs, n_in):
    h_ref, g_ref, w_ref = refs[:3]
    a_refs = refs[3:3 + n_in]
    o_ref, xn_ref = refs[3 + n_in:]
    acc = h_ref[...]
    row = 0
    for a_ref in a_refs:
        k = a_ref.shape[1]
        acc = acc + _dot(a_ref[...], w_ref[row:row + k, :])
        row += k
    o_ref[...] = acc
    xn_ref[...] = _rms(acc, g_ref[...]).astype(BF16)


def _resident(shape):
    return pl.BlockSpec(shape, lambda i: (0,) * len(shape), pipeline_mode=pl.Buffered(1))


def _proj_residual(h, g, a_list, w, tm):
    m, n = h.shape
    assert sum(a.shape[1] for a in a_list) == w.shape[0]
    row_block = pl.BlockSpec((tm, n), lambda i: (i, 0))
    in_specs = [row_block, _resident((1, n)), _resident(w.shape)]
    in_specs += [pl.BlockSpec((tm, a.shape[1]), lambda i: (i, 0)) for a in a_list]
    return pl.pallas_call(
        functools.partial(_proj_residual_kernel, n_in=len(a_list)),
        grid=(m // tm,),
        in_specs=in_specs,
        out_specs=[row_block, row_block],
        out_shape=[jax.ShapeDtypeStruct((m, n), F32), jax.ShapeDtypeStruct((m, n), BF16)],
        compiler_params=_params("parallel"),
        name="proj_residual",
    )(h, g.reshape(1, n), w, *a_list)


def _ffn_kernel(xn_ref, wg_ref, wu_ref, wd_ref, o_ref):
    @pl.when(pl.program_id(1) == 0)
    def _():
        o_ref[...] = jnp.zeros(o_ref.shape, F32)

    xn = xn_ref[...]
    gate = _dot(xn, wg_ref[...])
    up = _dot(xn, wu_ref[...])
    act = (gate * jax.nn.sigmoid(gate) * up).astype(BF16)
    o_ref[...] += _dot(act, wd_ref[...])


def _ffn(xn, wg, wu, wd, tm, th):
    m, d = xn.shape
    hid = wg.shape[1]
    return pl.pallas_call(
        _ffn_kernel,
        grid=(m // tm, hid // th),
        in_specs=[
            pl.BlockSpec((tm, d), lambda i, j: (i, 0)),
            pl.BlockSpec((d, th), lambda i, j: (0, j)),
            pl.BlockSpec((d, th), lambda i, j: (0, j)),
            pl.BlockSpec((th, d), lambda i, j: (j, 0)),
        ],
        out_specs=pl.BlockSpec((tm, d), lambda i, j: (i, 0)),
        out_shape=jax.ShapeDtypeStruct((m, d), F32),
        compiler_params=_params("parallel", "arbitrary"),
        name="ffn",
    )(xn, wg, wu, wd)


def _ple_kernel(x_ref, y_ref, g_ref, wg_ref, p_ref, wp_ref, ng_ref, *o_refs, last):
    x = x_ref[...] + y_ref[...]
    gate = jax.nn.sigmoid(_dot(_rms(x, g_ref[...]).astype(BF16), wg_ref[...]))
    out = x + _dot(p_ref[...].astype(BF16), wp_ref[...]) * gate
    if last:
        o_refs[0][...] = _rms(out, ng_ref[...])
    else:
        o_refs[0][...] = out
        o_refs[1][...] = _rms(out, ng_ref[...]).astype(BF16)


def _ple(h, y, g, wg, p, wp, next_g, tm, last):
    m, d = h.shape
    pd = p.shape[1]
    row_block = pl.BlockSpec((tm, d), lambda i: (i, 0))
    h_shape = jax.ShapeDtypeStruct((m, d), F32)
    return pl.pallas_call(
        functools.partial(_ple_kernel, last=last),
        grid=(m // tm,),
        in_specs=[
            row_block,
            row_block,
            _resident((1, d)),
            _resident(wg.shape),
            pl.BlockSpec((tm, pd), lambda i: (i, 0)),
            _resident(wp.shape),
            _resident((1, d)),
        ],
        out_specs=row_block if last else [row_block, row_block],
        out_shape=h_shape if last else [h_shape, jax.ShapeDtypeStruct((m, d), BF16)],
        compiler_params=_params("parallel"),
        name="ple",
    )(h, y, g.reshape(1, d), wg, p, wp, next_g.reshape(1, d))


def _key_to_float(key):
    bits = key ^ ((key >> 31) & 0x7FFFFFFF)
    return lax.bitcast_convert_type(bits, F32)


def _dsa_kernel(qi_ref, qsm_ref, ksm_ref, qa_ref, c_ref, o_ref,
                sc_ref, ct_ref, acc_ref, m_ref, l_ref, *, topk, qb_size, seq):
    t = qb_size
    qb = pl.program_id(1)
    nk = qb + 1

    @pl.when(qb == 0)
    def _():
        ct_ref[...] = jnp.transpose(c_ref[...].astype(F32)).astype(BF16)

    row = lax.broadcasted_iota(I32, (t, t), 0)
    col = lax.broadcasted_iota(I32, (t, t), 1)
    limit = (((qb * t + col) >> CHUNK_SHIFT) + 1) << CHUNK_SHIFT

    w_t = jnp.transpose(qsm_ref[...].astype(F32))[SM_W:SM_W + IDX_HEADS, :]

    def score_tile(kt, carry):
        rows = pl.ds(pl.multiple_of(kt * t, t), t)
        kk = ksm_ref[rows, :][:, :IDX_DIM]
        sc = jnp.zeros((t, t), F32)
        for h in range(IDX_HEADS):
            qh = qi_ref[:, h * IDX_DIM:(h + 1) * IDX_DIM]
            sc = sc + jnp.maximum(_dot_nt(kk, qh), 0.0) * w_t[h:h + 1, :]
        sc = sc * (IDX_DIM ** -0.5 * IDX_HEADS ** -0.5)
        sc_ref[rows, :] = jnp.where(kt * t + row < limit, sc, -jnp.inf)
        return carry

    lax.fori_loop(0, nk, score_tile, 0)

    def count(pred):
        def body(kt, acc):
            rows = pl.ds(pl.multiple_of(kt * t, t), t)
            hit = pred(sc_ref[rows, :]).astype(I32)
            return acc + jnp.sum(hit.reshape(t // SUBLANES, SUBLANES, t), axis=0)
        acc = lax.fori_loop(0, nk, body, jnp.zeros((SUBLANES, t), I32))
        return jnp.sum(acc, axis=0, keepdims=True)

    def count_ge(key):
        cand = _key_to_float(key)
        return count(lambda s: s >= cand)

    zero_key = jnp.zeros((1, t), I32)
    res0 = jnp.where(count_ge(zero_key) >= topk, 0, INT_MIN).astype(I32)

    def bisect(i, res):
        cand = res | jnp.left_shift(jnp.int32(1), 30 - i)
        return jnp.where(count_ge(cand) >= topk, cand, res)

    res = lax.fori_loop(0, 31, bisect, res0)
    thr = _key_to_float(jnp.maximum(res, KEY_NEG_INF))
    n_ge = count(lambda s: s >= thr) + jnp.where(thr == -jnp.inf, seq - nk * t, 0)

    @pl.when(jnp.max(n_ge) > topk)
    def _():
        need = (topk - count(lambda s: s > thr)).astype(F32)
        tri = (row >= col).astype(BF16)

        def body(kt, seen):
            rows = pl.ds(pl.multiple_of(kt * t, t), t)
            s = sc_ref[rows, :]
            eq = s == thr
            rank = _dot(tri, eq.astype(BF16)) + seen
            drop = (eq & (rank > need)) | (kt * t + row >= limit)
            sc_ref[rows, :] = jnp.where(drop, jnp.nan, s)
            return seen + jnp.sum(eq.astype(F32), axis=0, keepdims=True)

        lax.fori_loop(0, nk, body, jnp.zeros((1, t), F32))

    m_ref[...] = jnp.full(m_ref.shape, -1e30, F32)
    l_ref[...] = jnp.zeros(l_ref.shape, F32)
    acc_ref[...] = jnp.zeros(acc_ref.shape, F32)

    def attend(kt, carry):
        rows = pl.ds(pl.multiple_of(kt * t, t), t)
        valid = sc_ref[rows, :] >= thr
        ck = c_ref[rows, :]
        ckt = ct_ref[:, rows]
        c2 = (A_LATENT ** -0.5) * LOG2_E
        for g in range(0, A_HEADS, DSA_HEAD_GROUP):
            group = range(g, g + DSA_HEAD_GROUP)
            raws = [jnp.where(valid, _dot_nt(ck, qa_ref[:, h * A_LATENT:(h + 1) * A_LATENT]), -1e30)
                    for h in group]
            ps, alphas = [], []
            for h, raw in zip(group, raws):
                m_old = m_ref[h:h + 1, :]
                m_new = jnp.maximum(m_old, jnp.max(raw, axis=0, keepdims=True))
                alpha = jnp.exp2((m_old - m_new) * c2)
                p = jnp.exp2((raw - m_new) * c2)
                l_ref[h:h + 1, :] = alpha * l_ref[h:h + 1, :] + jnp.sum(p, axis=0, keepdims=True)
                m_ref[h:h + 1, :] = m_new
                ps.append(p.astype(BF16))
                alphas.append(alpha)
            pvs = [_dot(ckt, p) for p in ps]
            for h, alpha, pv in zip(group, alphas, pvs):
                hs = slice(h * A_LATENT, (h + 1) * A_LATENT)
                acc_ref[hs, :] = alpha * acc_ref[hs, :] + pv
        return carry

    lax.fori_loop(0, nk, attend, 0)

    for h in range(A_HEADS):
        hs = slice(h * A_LATENT, (h + 1) * A_LATENT)
        out = acc_ref[hs, :] / l_ref[h:h + 1, :]
        o_ref[:, hs] = jnp.transpose(out).astype(o_ref.dtype)


def _dsa(u, batch, seq, qb_size):
    m = batch * seq
    nqb = seq // qb_size
    topk = min(A_TOPK_MAX, seq // 4)
    wide = A_HEADS * A_LATENT
    return pl.pallas_call(
        functools.partial(_dsa_kernel, topk=topk, qb_size=qb_size, seq=seq),
        grid=(batch, nqb),
        in_specs=[
            pl.BlockSpec((qb_size, wide), lambda b, q: (b * nqb + q, U_QI // wide)),
            pl.BlockSpec((qb_size, LANES), lambda b, q: (b * nqb + q, U_SMALL // LANES)),
            pl.BlockSpec((seq, LANES), lambda b, q: (b, U_SMALL // LANES)),
            pl.BlockSpec((qb_size, wide), lambda b, q: (b * nqb + q, U_QA // wide)),
            pl.BlockSpec((seq, LANES), lambda b, q: (b, U_CA // LANES)),
        ],
        out_specs=pl.BlockSpec((qb_size, wide), lambda b, q: (b * nqb + q, 0)),
        out_shape=jax.ShapeDtypeStruct((m, wide), BF16),
        scratch_shapes=[
            pltpu.VMEM((seq, qb_size), F32),
            pltpu.VMEM((A_LATENT, seq), BF16),
            pltpu.VMEM((wide, qb_size), F32),
            pltpu.VMEM((A_HEADS, qb_size), F32),
            pltpu.VMEM((A_HEADS, qb_size), F32),
        ],
        compiler_params=_params("parallel", "arbitrary"),
        name="dsa",
    )(u, u, u, u, u)


def _gla_kernel(q_ref, k_ref, v_ref, g_ref, sm_ref, wgk_ref, bgk_ref, gn_ref, o_ref, st_ref,
                *, nchunks):
    c = CHUNK
    n = GLA_CHUNKS_PER_STEP
    span = n * c
    row = lax.broadcasted_iota(I32, (span, span), 0)
    col = lax.broadcasted_iota(I32, (span, span), 1)
    tri = ((row >= col) & ((row >> CHUNK_SHIFT) == (col >> CHUNK_SHIFT))).astype(BF16)
    causal = lax.broadcasted_iota(I32, (c, c), 0) >= lax.broadcasted_iota(I32, (c, c), 1)
    st_ref[...] = jnp.zeros(st_ref.shape, F32)
    wgk = wgk_ref[...]
    bgk = bgk_ref[...]
    gn = gn_ref[...]
    heads = range(B_HEADS)
    ksl = [slice(h * B_DK, (h + 1) * B_DK) for h in heads]
    vsl = [slice(h * B_DV, (h + 1) * B_DV) for h in heads]
    work = [(s, h) for s in range(n) for h in heads]

    def step(si, carry):
        base = pl.multiple_of(si * span, span)
        rows = pl.ds(base, span)
        x = _dot(sm_ref[rows, :], wgk) + bgk
        gk = -(jnp.maximum(-x, 0.0) + _softplus_tail(x)) / B_GATE_NORM
        hi, lo = _split_bf16(gk)
        cum_all = _dot(tri, hi) + _dot(tri, lo)
        q_es, k_es, k_ds, decays, vs_ = {}, {}, {}, {}, {}
        for s, h in work:
            sub = pl.ds(base + s * c, c)
            cum = cum_all[s * c:(s + 1) * c, ksl[h]]
            last = cum[c - 1:c, :]
            q = q_ref[sub, ksl[h]].astype(F32) * (B_DK ** -0.5)
            k = k_ref[sub, ksl[h]].astype(F32)
            q_es[s, h] = (q * jnp.exp(cum)).astype(BF16)
            k_es[s, h] = (k * jnp.exp(-cum)).astype(BF16)
            k_ds[s, h] = (k * jnp.exp(last - cum)).astype(BF16)
            decays[s, h] = jnp.exp(last)
            vs_[s, h] = v_ref[sub, vsl[h]]
        atts = {w: _dot_nt(q_es[w], k_es[w]) for w in work}
        kvs = {w: lax.dot_general(vs_[w], k_ds[w], TN_DIMS, preferred_element_type=F32)
               for w in work}
        sts = {h: st_ref[h] for h in heads}
        inter = {}
        for s in range(n):
            for h in heads:
                inter[s, h] = _dot_nt(q_es[s, h], sts[h].astype(BF16))
            for h in heads:
                sts[h] = sts[h] * decays[s, h] + kvs[s, h]
        for h in heads:
            st_ref[h] = sts[h]
        atts = {w: jnp.where(causal, atts[w], 0.0).astype(BF16) for w in work}
        intra = {w: _dot(atts[w], vs_[w]) for w in work}
        for s, h in work:
            sub = pl.ds(base + s * c, c)
            gate = g_ref[sub, vsl[h]].astype(F32)
            o = intra[s, h] + inter[s, h]
            o_ref[sub, vsl[h]] = (_rms(o, gn) * (gate * jax.nn.sigmoid(gate))).astype(o_ref.dtype)
        return carry

    lax.fori_loop(0, nchunks // n, step, 0)


def _gla(u, wgk_pad, bgk, gnorm, batch, seq):
    m = batch * seq
    qk_wide = B_HEADS * B_DK
    v_wide = B_HEADS * B_DV
    return pl.pallas_call(
        functools.partial(_gla_kernel, nchunks=seq // CHUNK),
        grid=(batch,),
        in_specs=[
            pl.BlockSpec((seq, qk_wide), lambda b: (b, U_QB // qk_wide)),
            pl.BlockSpec((seq, qk_wide), lambda b: (b, U_KB // qk_wide)),
            pl.BlockSpec((seq, v_wide), lambda b: (b, U_VB // v_wide)),
            pl.BlockSpec((seq, v_wide), lambda b: (b, U_GB // v_wide)),
            pl.BlockSpec((seq, LANES), lambda b: (b, U_SMALL // LANES)),
            pl.BlockSpec((LANES, qk_wide), lambda b: (0, 0)),
            pl.BlockSpec((1, qk_wide), lambda b: (0, 0)),
            pl.BlockSpec((1, B_DV), lambda b: (0, 0)),
        ],
        out_specs=pl.BlockSpec((seq, v_wide), lambda b: (b, 0)),
        out_shape=jax.ShapeDtypeStruct((m, v_wide), BF16),
        scratch_shapes=[pltpu.VMEM((B_HEADS, B_DV, B_DK), F32)],
        compiler_params=_params("parallel"),
        name="gla",
    )(u, u, u, u, u, wgk_pad, bgk.reshape(1, -1), gnorm.reshape(1, -1))


def _sb_kernel(q_ref, k_ref, v_ref, o_ref, *, blk, nblk, head_dim, heads):
    t = blk
    d = head_dim
    row = lax.broadcasted_iota(I32, (t, t), 0)
    col = lax.broadcasted_iota(I32, (t, t), 1)
    upper = (row > col).astype(BF16)
    diag = col < row
    scale = d ** -0.5

    hsl = [slice(h * d, (h + 1) * d) for h in range(heads)]

    def block(qrows, krows, runs, mask):
        zs = [_dot_nt(q_ref[qrows, hs], k_ref[krows, hs]) * scale for hs in hsl]
        lms, log_bs = [], []
        for z in zs:
            log_1mb = -(jnp.maximum(z, 0.0) + jnp.log(1.0 + jnp.exp(-jnp.abs(z))))
            log_bs.append(log_1mb + z)
            lms.append(log_1mb if mask is None else jnp.where(mask, log_1mb, 0.0))
        sufs = [_dot(lm.astype(BF16), upper) for lm in lms]
        ws = []
        for suf, log_b, run in zip(sufs, log_bs, runs):
            w = jnp.exp(suf + run + log_b)
            ws.append((w if mask is None else jnp.where(mask, w, 0.0)).astype(BF16))
        outs = [_dot(w, v_ref[krows, hs]) for w, hs in zip(ws, hsl)]
        new_runs = [run + jnp.sum(lm, axis=1, keepdims=True) for run, lm in zip(runs, lms)]
        return new_runs, outs

    def q_block(qb, carry):
        qrows = pl.ds(pl.multiple_of(qb * t, t), t)
        runs, accs = block(qrows, qrows, [jnp.zeros((t, 1), F32)] * heads, diag)

        def alive(rs):
            top = jnp.max(rs[0])
            for r in rs[1:]:
                top = jnp.maximum(top, jnp.max(r))
            return (top >= SB_EXP_UNDERFLOW).astype(I32)

        def cond(state):
            i, live, _, _ = state
            return jnp.logical_and(i <= qb, live > 0)

        def k_block(state):
            i, _, rs, acs = state
            krows = pl.ds(pl.multiple_of((qb - i) * t, t), t)
            new_rs, outs = block(qrows, krows, list(rs), None)
            new_acs = [a + o for a, o in zip(acs, outs)]
            return i + 1, alive(new_rs), tuple(new_rs), tuple(new_acs)

        state = (jnp.int32(1), alive(runs), tuple(runs), tuple(accs))
        _, _, _, accs = lax.while_loop(cond, k_block, state)
        for h in range(heads):
            o_ref[qrows, h * d:(h + 1) * d] = accs[h].astype(o_ref.dtype)
        return carry

    lax.fori_loop(0, nblk, q_block, 0)


def _stick_breaking(qkv, batch, seq, blk):
    m = batch * seq
    d = qkv.shape[1] // (3 * C_HEADS)
    groups = C_HEADS // SB_HEADS_PER_STEP
    wide = SB_HEADS_PER_STEP * d
    return pl.pallas_call(
        functools.partial(_sb_kernel, blk=blk, nblk=seq // blk, head_dim=d, heads=SB_HEADS_PER_STEP),
        grid=(batch, groups),
        in_specs=[
            pl.BlockSpec((seq, wide), lambda b, g: (b, g)),
            pl.BlockSpec((seq, wide), lambda b, g: (b, groups + g)),
            pl.BlockSpec((seq, wide), lambda b, g: (b, 2 * groups + g)),
        ],
        out_specs=pl.BlockSpec((seq, wide), lambda b, g: (b, g)),
        out_shape=jax.ShapeDtypeStruct((m, C_HEADS * d), BF16),
        compiler_params=_params("parallel", "arbitrary"),
        name="stick_breaking",
    )(qkv, qkv, qkv)


def _permute_even_w_in(w):
    d = w.shape[0]
    qa, ca, qi, ki, wi, qb, kb, vb, gb, gk = jnp.split(
        w, [1024, 1152, 2176, 2240, 2256, 2768, 3280, 4304, 5328], axis=1)
    pad = jnp.zeros((d, LANES - IDX_DIM - IDX_HEADS - B_GATE_RANK), w.dtype)
    return jnp.concatenate([qa, qi, vb, gb, qb, kb, ca, ki, wi, gk, pad], axis=1)


def kernel(x, p, even_norm, even_w_in, even_w_gk, even_b_gk, even_gla_norm, even_w_out,
           odd_norm, odd_w_in, odd_w_out, ffn_norm, ffn_w_gate, ffn_w_up, ffn_w_down,
           ple_norm, ple_w_gate, ple_w_proj, final_norm):
    batch, seq, d = x.shape
    depth = p.shape[0]
    m = batch * seq
    h = x.reshape(m, d)
    tm = _tile(m, 1024)
    tm_row = _tile(m, 512)
    blk = _tile(seq, 256)

    def mixer_norm(i):
        return even_norm[i // 2] if i % 2 == 0 else odd_norm[i // 2]

    def in_proj(h, hn, g, w, tn):
        return _norm_matmul(h, g, w, tm, tn) if hn is None else _matmul(hn, w, tm, tn)

    hn = None
    for i in range(depth):
        j = i // 2
        if i % 2 == 0:
            w_in = _permute_even_w_in(even_w_in[j].astype(BF16))
            u = in_proj(h, hn, even_norm[j], w_in, _tile(U_WIDTH, 768))
            o_a = _dsa(u, batch, seq, blk)
            wgk_pad = jnp.zeros((LANES, B_HEADS * B_DK), F32)
            wgk_pad = wgk_pad.at[SM_GK:SM_GK + B_GATE_RANK].set(even_w_gk[j]).astype(BF16)
            o_b = _gla(u, wgk_pad, even_b_gk[j], even_gla_norm[j], batch, seq)
            h, xn = _proj_residual(h, ffn_norm[i], [o_a, o_b], even_w_out[j].astype(BF16), tm_row)
        else:
            qkv = in_proj(h, hn, odd_norm[j], odd_w_in[j].astype(BF16), _tile(3 * d, 768))
            o = _stick_breaking(qkv, batch, seq, blk)
            h, xn = _proj_residual(h, ffn_norm[i], [o], odd_w_out[j].astype(BF16), tm_row)
        y = _ffn(xn, ffn_w_gate[i].astype(BF16), ffn_w_up[i].astype(BF16),
                 ffn_w_down[i].astype(BF16), tm, _tile(ffn_w_gate.shape[2], 512))
        last = i == depth - 1
        out = _ple(h, y, ple_norm[i], ple_w_gate[i].astype(BF16), p[i].reshape(m, -1),
                   ple_w_proj[i].astype(BF16), final_norm if last else mixer_norm(i + 1),
                   tm_row, last)
        h, hn = (out, None) if last else out
    return h.reshape(batch, seq, d)
```

```python
import functools

import jax
import jax.numpy as jnp
from jax import lax
from jax.experimental import pallas as pl
from jax.experimental.pallas import tpu as pltpu

F32 = jnp.float32
BF16 = jnp.bfloat16
I32 = jnp.int32

EPS = 1e-6
CHUNK = 64
CHUNK_SHIFT = CHUNK.bit_length() - 1
A_HEADS = 8
A_LATENT = 128
A_TOPK_MAX = 256
IDX_HEADS = 16
IDX_DIM = 64
B_HEADS = 4
B_DK = 128
B_DV = 256
B_GATE_RANK = 16
B_GATE_NORM = 16.0
GLA_CHUNKS_PER_STEP = 4
C_HEADS = 16
LOG2_E = 1.4426950408889634
SB_HEADS_PER_STEP = 4
SB_EXP_UNDERFLOW = -104.0

LANES = 128
SUBLANES = 8
VMEM_LIMIT_BYTES = 56 * 1024 * 1024

U_QA = 0
U_QI = 1024
U_VB = 2048
U_GB = 3072
U_QB = 4096
U_KB = 4608
U_CA = 5120
U_SMALL = 5248
U_WIDTH = 5376
SM_W = 64
SM_GK = 80

NT_DIMS = (((1,), (1,)), ((), ()))
TN_DIMS = (((0,), (0,)), ((), ()))

KEY_NEG_INF = -2139095041
INT_MIN = -2147483648


def _params(*sem):
    return pltpu.CompilerParams(dimension_semantics=sem, vmem_limit_bytes=VMEM_LIMIT_BYTES)


def _tile(n, pref):
    if n <= pref:
        return n
    t = (pref // LANES) * LANES
    while n % t:
        t -= LANES
    return t


def _rms(x, g):
    ms = jnp.mean(x * x, axis=-1, keepdims=True)
    return x * lax.rsqrt(ms + EPS) * g


def _dot(a, b):
    return jnp.dot(a, b, preferred_element_type=F32)


def _dot_nt(a, b):
    return lax.dot_general(a, b, NT_DIMS, preferred_element_type=F32)


def _softplus_tail(z):
    return jnp.log1p(jnp.exp(-jnp.abs(z)))


def _split_bf16(x):
    hi = x.astype(BF16)
    lo = (x - hi.astype(F32)).astype(BF16)
    return hi, lo


def _norm_matmul_kernel(x_ref, g_ref, w_ref, o_ref, xn_ref):
    @pl.when(pl.program_id(1) == 0)
    def _():
        xn_ref[...] = _rms(x_ref[...], g_ref[...]).astype(BF16)

    o_ref[...] = _dot(xn_ref[...], w_ref[...]).astype(o_ref.dtype)


def _norm_matmul(x, g, w, tm, tn):
    m, k = x.shape
    n = w.shape[1]
    return pl.pallas_call(
        _norm_matmul_kernel,
        grid=(m // tm, n // tn),
        in_specs=[
            pl.BlockSpec((tm, k), lambda i, j: (i, 0)),
            pl.BlockSpec((1, k), lambda i, j: (0, 0)),
            pl.BlockSpec((k, tn), lambda i, j: (0, j)),
        ],
        out_specs=pl.BlockSpec((tm, tn), lambda i, j: (i, j)),
        out_shape=jax.ShapeDtypeStruct((m, n), BF16),
        scratch_shapes=[pltpu.VMEM((tm, k), BF16)],
        compiler_params=_params("parallel", "arbitrary"),
        name="norm_matmul",
    )(x, g.reshape(1, k), w)


def _matmul_kernel(x_ref, w_ref, o_ref):
    o_ref[...] = _dot(x_ref[...], w_ref[...]).astype(o_ref.dtype)


def _matmul(xn, w, tm, tn):
    m, k = xn.shape
    n = w.shape[1]
    return pl.pallas_call(
        _matmul_kernel,
        grid=(m // tm, n // tn),
        in_specs=[
            pl.BlockSpec((tm, k), lambda i, j: (i, 0)),
            pl.BlockSpec((k, tn), lambda i, j: (0, j)),
        ],
        out_specs=pl.BlockSpec((tm, tn), lambda i, j: (i, j)),
        out_shape=jax.ShapeDtypeStruct((m, n), BF16),
        compiler_params=_params("parallel", "arbitrary"),
        name="matmul",
    )(xn, w)


def _proj_residual_kernel(*refs, n_in):
    h_ref, g_ref, w_ref = refs[:3]
    a_refs = refs[3:3 + n_in]
    o_ref, xn_ref = refs[3 + n_in:]
    acc = h_ref[...]
    row = 0
    for a_ref in a_refs:
        k = a_ref.shape[1]
        acc = acc + _dot(a_ref[...], w_ref[row:row + k, :])
        row += k
    o_ref[...] = acc
    xn_ref[...] = _rms(acc, g_ref[...]).astype(BF16)


def _resident(shape):
    return pl.BlockSpec(shape, lambda i: (0,) * len(shape), pipeline_mode=pl.Buffered(1))


def _proj_residual(h, g, a_list, w, tm):
    m, n = h.shape
    assert sum(a.shape[1] for a in a_list) == w.shape[0]
    row_block = pl.BlockSpec((tm, n), lambda i: (i, 0))
    in_specs = [row_block, _resident((1, n)), _resident(w.shape)]
    in_specs += [pl.BlockSpec((tm, a.shape[1]), lambda i: (i, 0)) for a in a_list]
    return pl.pallas_call(
        functools.partial(_proj_residual_kernel, n_in=len(a_list)),
        grid=(m // tm,),
        in_specs=in_specs,
        out_specs=[row_block, row_block],
        out_shape=[jax.ShapeDtypeStruct((m, n), F32), jax.ShapeDtypeStruct((m, n), BF16)],
        compiler_params=_params("parallel"),
        name="proj_residual",
    )(h, g.reshape(1, n), w, *a_list)


def _ffn_kernel(xn_ref, wg_ref, wu_ref, wd_ref, o_ref):
    @pl.when(pl.program_id(1) == 0)
    def _():
        o_ref[...] = jnp.zeros(o_ref.shape, F32)

    xn = xn_ref[...]
    gate = _dot(xn, wg_ref[...])
    up = _dot(xn, wu_ref[...])
    act = (gate * jax.nn.sigmoid(gate) * up).astype(BF16)
    o_ref[...] += _dot(act, wd_ref[...])


def _ffn(xn, wg, wu, wd, tm, th):
    m, d = xn.shape
    hid = wg.shape[1]
    return pl.pallas_call(
        _ffn_kernel,
        grid=(m // tm, hid // th),
        in_specs=[
            pl.BlockSpec((tm, d), lambda i, j: (i, 0)),
            pl.BlockSpec((d, th), lambda i, j: (0, j)),
            pl.BlockSpec((d, th), lambda i, j: (0, j)),
            pl.BlockSpec((th, d), lambda i, j: (j, 0)),
        ],
        out_specs=pl.BlockSpec((tm, d), lambda i, j: (i, 0)),
        out_shape=jax.ShapeDtypeStruct((m, d), F32),
        compiler_params=_params("parallel", "arbitrary"),
        name="ffn",
    )(xn, wg, wu, wd)


def _ple_kernel(x_ref, y_ref, g_ref, wg_ref, p_ref, wp_ref, ng_ref, *o_refs, last):
    x = x_ref[...] + y_ref[...]
    gate = jax.nn.sigmoid(_dot(_rms(x, g_ref[...]).astype(BF16), wg_ref[...]))
    out = x + _dot(p_ref[...].astype(BF16), wp_ref[...]) * gate
    if last:
        o_refs[0][...] = _rms(out, ng_ref[...])
    else:
        o_refs[0][...] = out
        o_refs[1][...] = _rms(out, ng_ref[...]).astype(BF16)


def _ple(h, y, g, wg, p, layer, wp, next_g, tm, last):
    m, d = h.shape
    pd = p.shape[2]
    row_block = pl.BlockSpec((tm, d), lambda i: (i, 0))
    h_shape = jax.ShapeDtypeStruct((m, d), F32)
    return pl.pallas_call(
        functools.partial(_ple_kernel, last=last),
        grid=(m // tm,),
        in_specs=[
            row_block,
            row_block,
            _resident((1, d)),
            _resident(wg.shape),
            pl.BlockSpec((None, tm, pd), lambda i: (layer, i, 0)),
            _resident(wp.shape),
            _resident((1, d)),
        ],
        out_specs=row_block if last else [row_block, row_block],
        out_shape=h_shape if last else [h_shape, jax.ShapeDtypeStruct((m, d), BF16)],
        compiler_params=_params("parallel"),
        name="ple",
    )(h, y, g.reshape(1, d), wg, p, wp, next_g.reshape(1, d))


def _key_to_float(key):
    bits = key ^ ((key >> 31) & 0x7FFFFFFF)
    return lax.bitcast_convert_type(bits, F32)


def _dsa_kernel(qi_ref, qsm_ref, ksm_ref, qa_ref, c_ref, o_ref,
                sc_ref, ct_ref, acc_ref, m_ref, l_ref, *, topk, qb_size, seq):
    t = qb_size
    qb = pl.program_id(1)
    nk = qb + 1

    @pl.when(qb == 0)
    def _():
        ct_ref[...] = jnp.transpose(c_ref[...].astype(F32)).astype(BF16)

    row = lax.broadcasted_iota(I32, (t, t), 0)
    col = lax.broadcasted_iota(I32, (t, t), 1)
    limit = (((qb * t + col) >> CHUNK_SHIFT) + 1) << CHUNK_SHIFT

    w_t = jnp.transpose(qsm_ref[...].astype(F32))[SM_W:SM_W + IDX_HEADS, :]

    def score_tile(kt, carry):
        rows = pl.ds(pl.multiple_of(kt * t, t), t)
        kk = ksm_ref[rows, :][:, :IDX_DIM]
        sc = jnp.zeros((t, t), F32)
        for h in range(IDX_HEADS):
            qh = qi_ref[:, h * IDX_DIM:(h + 1) * IDX_DIM]
            sc = sc + jnp.maximum(_dot_nt(kk, qh), 0.0) * w_t[h:h + 1, :]
        sc = sc * (IDX_DIM ** -0.5 * IDX_HEADS ** -0.5)
        sc_ref[rows, :] = jnp.where(kt * t + row < limit, sc, -jnp.inf)
        return carry

    lax.fori_loop(0, nk, score_tile, 0)

    def count(pred):
        def body(kt, acc):
            rows = pl.ds(pl.multiple_of(kt * t, t), t)
            hit = pred(sc_ref[rows, :]).astype(I32)
            return acc + jnp.sum(hit.reshape(t // SUBLANES, SUBLANES, t), axis=0)
        acc = lax.fori_loop(0, nk, body, jnp.zeros((SUBLANES, t), I32))
        return jnp.sum(acc, axis=0, keepdims=True)

    def count_ge(key):
        cand = _key_to_float(key)
        return count(lambda s: s >= cand)

    zero_key = jnp.zeros((1, t), I32)
    res0 = jnp.where(count_ge(zero_key) >= topk, 0, INT_MIN).astype(I32)

    def bisect(i, res):
        cand = res | jnp.left_shift(jnp.int32(1), 30 - i)
        return jnp.where(count_ge(cand) >= topk, cand, res)

    res = lax.fori_loop(0, 31, bisect, res0)
    thr = _key_to_float(jnp.maximum(res, KEY_NEG_INF))
    n_ge = count(lambda s: s >= thr) + jnp.where(thr == -jnp.inf, seq - nk * t, 0)

    @pl.when(jnp.max(n_ge) > topk)
    def _():
        need = (topk - count(lambda s: s > thr)).astype(F32)
        tri = (row >= col).astype(BF16)

        def body(kt, seen):
            rows = pl.ds(pl.multiple_of(kt * t, t), t)
            s = sc_ref[rows, :]
            eq = s == thr
            rank = _dot(tri, eq.astype(BF16)) + seen
            drop = (eq & (rank > need)) | (kt * t + row >= limit)
            sc_ref[rows, :] = jnp.where(drop, jnp.nan, s)
            return seen + jnp.sum(eq.astype(F32), axis=0, keepdims=True)

        lax.fori_loop(0, nk, body, jnp.zeros((1, t), F32))

    m_ref[...] = jnp.full(m_ref.shape, -1e30, F32)
    l_ref[...] = jnp.zeros(l_ref.shape, F32)
    acc_ref[...] = jnp.zeros(acc_ref.shape, F32)

    heads = range(A_HEADS)
    hsl = [slice(h * A_LATENT, (h + 1) * A_LATENT) for h in heads]
    c2 = (A_LATENT ** -0.5) * LOG2_E

    def attend(kt, carry):
        rows = pl.ds(pl.multiple_of(kt * t, t), t)
        valid = sc_ref[rows, :] >= thr
        ck = c_ref[rows, :]
        ckt = ct_ref[:, rows]
        raws = [jnp.where(valid, _dot_nt(ck, qa_ref[:, hs]), -1e30) for hs in hsl]
        ps, alphas = [], []
        for h, raw in zip(heads, raws):
            m_old = m_ref[h:h + 1, :]
            m_new = jnp.maximum(m_old, jnp.max(raw, axis=0, keepdims=True))
            alpha = jnp.exp2((m_old - m_new) * c2)
            p = jnp.exp2((raw - m_new) * c2)
            l_ref[h:h + 1, :] = alpha * l_ref[h:h + 1, :] + jnp.sum(p, axis=0, keepdims=True)
            m_ref[h:h + 1, :] = m_new
            ps.append(p.astype(BF16))
            alphas.append(alpha)
        pvs = [_dot(ckt, p) for p in ps]
        for hs, alpha, pv in zip(hsl, alphas, pvs):
            acc_ref[hs, :] = alpha * acc_ref[hs, :] + pv
        return carry

    lax.fori_loop(0, nk, attend, 0)

    for h in heads:
        out = acc_ref[hsl[h], :] / l_ref[h:h + 1, :]
        o_ref[:, hsl[h]] = jnp.transpose(out).astype(o_ref.dtype)


def _dsa(u, batch, seq, qb_size):
    m = batch * seq
    nqb = seq // qb_size
    topk = min(A_TOPK_MAX, seq // 4)
    wide = A_HEADS * A_LATENT
    return pl.pallas_call(
        functools.partial(_dsa_kernel, topk=topk, qb_size=qb_size, seq=seq),
        grid=(batch, nqb),
        in_specs=[
            pl.BlockSpec((qb_size, wide), lambda b, q: (b * nqb + q, U_QI // wide)),
            pl.BlockSpec((qb_size, LANES), lambda b, q: (b * nqb + q, U_SMALL // LANES)),
            pl.BlockSpec((seq, LANES), lambda b, q: (b, U_SMALL // LANES)),
            pl.BlockSpec((qb_size, wide), lambda b, q: (b * nqb + q, U_QA // wide)),
            pl.BlockSpec((seq, LANES), lambda b, q: (b, U_CA // LANES)),
        ],
        out_specs=pl.BlockSpec((qb_size, wide), lambda b, q: (b * nqb + q, 0)),
        out_shape=jax.ShapeDtypeStruct((m, wide), BF16),
        scratch_shapes=[
            pltpu.VMEM((seq, qb_size), F32),
            pltpu.VMEM((A_LATENT, seq), BF16),
            pltpu.VMEM((wide, qb_size), F32),
            pltpu.VMEM((A_HEADS, qb_size), F32),
            pltpu.VMEM((A_HEADS, qb_size), F32),
        ],
        compiler_params=_params("parallel", "arbitrary"),
        name="dsa",
    )(u, u, u, u, u)


def _gla_kernel(q_ref, k_ref, v_ref, g_ref, sm_ref, wgk_ref, bgk_ref, gn_ref, o_ref, st_ref,
                *, nchunks):
    c = CHUNK
    n = GLA_CHUNKS_PER_STEP
    span = n * c
    row = lax.broadcasted_iota(I32, (span, span), 0)
    col = lax.broadcasted_iota(I32, (span, span), 1)
    tri = ((row >= col) & ((row >> CHUNK_SHIFT) == (col >> CHUNK_SHIFT))).astype(BF16)
    causal = lax.broadcasted_iota(I32, (c, c), 0) >= lax.broadcasted_iota(I32, (c, c), 1)
    st_ref[...] = jnp.zeros(st_ref.shape, F32)
    wgk = wgk_ref[...]
    bgk = bgk_ref[...]
    gn = gn_ref[...]
    heads = range(B_HEADS)
    ksl = [slice(h * B_DK, (h + 1) * B_DK) for h in heads]
    vsl = [slice(h * B_DV, (h + 1) * B_DV) for h in heads]
    work = [(s, h) for s in range(n) for h in heads]

    def step(si, carry):
        base = pl.multiple_of(si * span, span)
        rows = pl.ds(base, span)
        x = _dot(sm_ref[rows, :], wgk) + bgk
        gk = -(jnp.maximum(-x, 0.0) + _softplus_tail(x)) / B_GATE_NORM
        hi, lo = _split_bf16(gk)
        cum_all = _dot(tri, hi) + _dot(tri, lo)
        q_es, k_es, k_ds, decays, vs_ = {}, {}, {}, {}, {}
        for s, h in work:
            sub = pl.ds(base + s * c, c)
            cum = cum_all[s * c:(s + 1) * c, ksl[h]]
            last = cum[c - 1:c, :]
            q = q_ref[sub, ksl[h]].astype(F32) * (B_DK ** -0.5)
            k = k_ref[sub, ksl[h]].astype(F32)
            q_es[s, h] = (q * jnp.exp(cum)).astype(BF16)
            k_es[s, h] = (k * jnp.exp(-cum)).astype(BF16)
            k_ds[s, h] = (k * jnp.exp(last - cum)).astype(BF16)
            decays[s, h] = jnp.exp(last)
            vs_[s, h] = v_ref[sub, vsl[h]]
        atts = {w: _dot_nt(q_es[w], k_es[w]) for w in work}
        kvs = {w: lax.dot_general(vs_[w], k_ds[w], TN_DIMS, preferred_element_type=F32)
               for w in work}
        sts = {h: st_ref[h] for h in heads}
        inter = {}
        for s in range(n):
            for h in heads:
                inter[s, h] = _dot_nt(q_es[s, h], sts[h].astype(BF16))
            for h in heads:
                sts[h] = sts[h] * decays[s, h] + kvs[s, h]
        for h in heads:
            st_ref[h] = sts[h]
        atts = {w: jnp.where(causal, atts[w], 0.0).astype(BF16) for w in work}
        intra = {w: _dot(atts[w], vs_[w]) for w in work}
        for s, h in work:
            sub = pl.ds(base + s * c, c)
            gate = g_ref[sub, vsl[h]].astype(F32)
            o = intra[s, h] + inter[s, h]
            o_ref[sub, vsl[h]] = (_rms(o, gn) * (gate * jax.nn.sigmoid(gate))).astype(o_ref.dtype)
        return carry

    lax.fori_loop(0, nchunks // n, step, 0)


def _gla(u, wgk_pad, bgk, gnorm, batch, seq):
    m = batch * seq
    qk_wide = B_HEADS * B_DK
    v_wide = B_HEADS * B_DV
    return pl.pallas_call(
        functools.partial(_gla_kernel, nchunks=seq // CHUNK),
        grid=(batch,),
        in_specs=[
            pl.BlockSpec((seq, qk_wide), lambda b: (b, U_QB // qk_wide)),
            pl.BlockSpec((seq, qk_wide), lambda b: (b, U_KB // qk_wide)),
            pl.BlockSpec((seq, v_wide), lambda b: (b, U_VB // v_wide)),
            pl.BlockSpec((seq, v_wide), lambda b: (b, U_GB // v_wide)),
            pl.BlockSpec((seq, LANES), lambda b: (b, U_SMALL // LANES)),
            pl.BlockSpec((LANES, qk_wide), lambda b: (0, 0)),
            pl.BlockSpec((1, qk_wide), lambda b: (0, 0)),
            pl.BlockSpec((1, B_DV), lambda b: (0, 0)),
        ],
        out_specs=pl.BlockSpec((seq, v_wide), lambda b: (b, 0)),
        out_shape=jax.ShapeDtypeStruct((m, v_wide), BF16),
        scratch_shapes=[pltpu.VMEM((B_HEADS, B_DV, B_DK), F32)],
        compiler_params=_params("parallel"),
        name="gla",
    )(u, u, u, u, u, wgk_pad, bgk.reshape(1, -1), gnorm.reshape(1, -1))


def _sb_kernel(q_ref, k_ref, v_ref, o_ref, *, blk, nblk, head_dim, heads):
    t = blk
    d = head_dim
    row = lax.broadcasted_iota(I32, (t, t), 0)
    col = lax.broadcasted_iota(I32, (t, t), 1)
    upper = (row > col).astype(BF16)
    diag = col < row
    scale = d ** -0.5

    hsl = [slice(h * d, (h + 1) * d) for h in range(heads)]

    def block(qrows, krows, runs, mask):
        zs = [_dot_nt(q_ref[qrows, hs], k_ref[krows, hs]) * scale for hs in hsl]
        lms, log_bs = [], []
        for z in zs:
            log_1mb = -(jnp.maximum(z, 0.0) + jnp.log(1.0 + jnp.exp(-jnp.abs(z))))
            log_bs.append(log_1mb + z)
            lms.append(log_1mb if mask is None else jnp.where(mask, log_1mb, 0.0))
        sufs = [_dot(lm.astype(BF16), upper) for lm in lms]
        ws = []
        for suf, log_b, run in zip(sufs, log_bs, runs):
            w = jnp.exp(suf + run + log_b)
            ws.append((w if mask is None else jnp.where(mask, w, 0.0)).astype(BF16))
        outs = [_dot(w, v_ref[krows, hs]) for w, hs in zip(ws, hsl)]
        new_runs = [run + jnp.sum(lm, axis=1, keepdims=True) for run, lm in zip(runs, lms)]
        return new_runs, outs

    def q_block(qb, carry):
        qrows = pl.ds(pl.multiple_of(qb * t, t), t)
        runs, accs = block(qrows, qrows, [jnp.zeros((t, 1), F32)] * heads, diag)

        def alive(rs):
            top = jnp.max(rs[0])
            for r in rs[1:]:
                top = jnp.maximum(top, jnp.max(r))
            return (top >= SB_EXP_UNDERFLOW).astype(I32)

        def cond(state):
            i, live, _, _ = state
            return jnp.logical_and(i <= qb, live > 0)

        def k_block(state):
            i, _, rs, acs = state
            krows = pl.ds(pl.multiple_of((qb - i) * t, t), t)
            new_rs, outs = block(qrows, krows, list(rs), None)
            new_acs = [a + o for a, o in zip(acs, outs)]
            return i + 1, alive(new_rs), tuple(new_rs), tuple(new_acs)

        state = (jnp.int32(1), alive(runs), tuple(runs), tuple(accs))
        _, _, _, accs = lax.while_loop(cond, k_block, state)
        for h in range(heads):
            o_ref[qrows, h * d:(h + 1) * d] = accs[h].astype(o_ref.dtype)
        return carry

    lax.fori_loop(0, nblk, q_block, 0)


def _stick_breaking(qkv, batch, seq, blk):
    m = batch * seq
    d = qkv.shape[1] // (3 * C_HEADS)
    groups = C_HEADS // SB_HEADS_PER_STEP
    wide = SB_HEADS_PER_STEP * d
    return pl.pallas_call(
        functools.partial(_sb_kernel, blk=blk, nblk=seq // blk, head_dim=d, heads=SB_HEADS_PER_STEP),
        grid=(batch, groups),
        in_specs=[
            pl.BlockSpec((seq, wide), lambda b, g: (b, g)),
            pl.BlockSpec((seq, wide), lambda b, g: (b, groups + g)),
            pl.BlockSpec((seq, wide), lambda b, g: (b, 2 * groups + g)),
        ],
        out_specs=pl.BlockSpec((seq, wide), lambda b, g: (b, g)),
        out_shape=jax.ShapeDtypeStruct((m, C_HEADS * d), BF16),
        compiler_params=_params("parallel", "arbitrary"),
        name="stick_breaking",
    )(qkv, qkv, qkv)


def _permute_even_w_in(w):
    d = w.shape[0]
    qa, ca, qi, ki, wi, qb, kb, vb, gb, gk = jnp.split(
        w, [1024, 1152, 2176, 2240, 2256, 2768, 3280, 4304, 5328], axis=1)
    pad = jnp.zeros((d, LANES - IDX_DIM - IDX_HEADS - B_GATE_RANK), w.dtype)
    return jnp.concatenate([qa, qi, vb, gb, qb, kb, ca, ki, wi, gk, pad], axis=1)


def kernel(x, p, even_norm, even_w_in, even_w_gk, even_b_gk, even_gla_norm, even_w_out,
           odd_norm, odd_w_in, odd_w_out, ffn_norm, ffn_w_gate, ffn_w_up, ffn_w_down,
           ple_norm, ple_w_gate, ple_w_proj, final_norm):
    batch, seq, d = x.shape
    depth = p.shape[0]
    m = batch * seq
    h = x.reshape(m, d)
    tm = _tile(m, 1024)
    tm_row = _tile(m, 512)
    blk = _tile(seq, 256)

    def mixer_norm(i):
        return even_norm[i // 2] if i % 2 == 0 else odd_norm[i // 2]

    def in_proj(h, hn, g, w, tn):
        return _norm_matmul(h, g, w, tm, tn) if hn is None else _matmul(hn, w, tm, tn)

    hn = None
    for i in range(depth):
        j = i // 2
        if i % 2 == 0:
            w_in = _permute_even_w_in(even_w_in[j].astype(BF16))
            u = in_proj(h, hn, even_norm[j], w_in, _tile(U_WIDTH, 768))
            o_a = _dsa(u, batch, seq, blk)
            wgk_pad = jnp.zeros((LANES, B_HEADS * B_DK), F32)
            wgk_pad = wgk_pad.at[SM_GK:SM_GK + B_GATE_RANK].set(even_w_gk[j]).astype(BF16)
            o_b = _gla(u, wgk_pad, even_b_gk[j], even_gla_norm[j], batch, seq)
            h, xn = _proj_residual(h, ffn_norm[i], [o_a, o_b], even_w_out[j].astype(BF16), tm_row)
        else:
            qkv = in_proj(h, hn, odd_norm[j], odd_w_in[j].astype(BF16), _tile(3 * d, 768))
            o = _stick_breaking(qkv, batch, seq, blk)
            h, xn = _proj_residual(h, ffn_norm[i], [o], odd_w_out[j].astype(BF16), tm_row)
        y = _ffn(xn, ffn_w_gate[i].astype(BF16), ffn_w_up[i].astype(BF16),
                 ffn_w_down[i].astype(BF16), tm, _tile(ffn_w_gate.shape[2], 512))
        last = i == depth - 1
        out = _ple(h, y, ple_norm[i], ple_w_gate[i].astype(BF16), p.reshape(depth, m, -1), i,
                   ple_w_proj[i].astype(BF16), final_norm if last else mixer_norm(i + 1),
                   tm_row, last)
        h, hn = (out, None) if last else out
    return h.reshape(batch, seq, d)
```

```python
import functools

import jax
import jax.numpy as jnp
from jax import lax
from jax.experimental import pallas as pl
from jax.experimental.pallas import tpu as pltpu

F32 = jnp.float32
BF16 = jnp.bfloat16
I32 = jnp.int32

EPS = 1e-6
CHUNK = 64
CHUNK_SHIFT = CHUNK.bit_length() - 1
A_HEADS = 8
A_LATENT = 128
A_TOPK_MAX = 256
IDX_HEADS = 16
IDX_DIM = 64
B_HEADS = 4
B_DK = 128
B_DV = 256
B_GATE_RANK = 16
B_GATE_NORM = 16.0
GLA_CHUNKS_PER_STEP = 4
C_HEADS = 16
LOG2_E = 1.4426950408889634
SB_HEADS_PER_STEP = 4
SB_EXP_UNDERFLOW = -104.0

LANES = 128
SUBLANES = 8
VMEM_LIMIT_BYTES = 56 * 1024 * 1024

U_QA = 0
U_QI = 1024
U_VB = 2048
U_GB = 3072
U_QB = 4096
U_KB = 4608
U_CA = 5120
U_SMALL = 5248
U_WIDTH = 5376
SM_W = 64
SM_GK = 80

NT_DIMS = (((1,), (1,)), ((), ()))
TN_DIMS = (((0,), (0,)), ((), ()))

KEY_NEG_INF = -2139095041
INT_MIN = -2147483648


def _params(*sem):
    return pltpu.CompilerParams(dimension_semantics=sem, vmem_limit_bytes=VMEM_LIMIT_BYTES)


def _tile(n, pref):
    if n <= pref:
        return n
    t = (pref // LANES) * LANES
    while n % t:
        t -= LANES
    return t


def _rms(x, g):
    ms = jnp.mean(x * x, axis=-1, keepdims=True)
    return x * lax.rsqrt(ms + EPS) * g


def _dot(a, b):
    return jnp.dot(a, b, preferred_element_type=F32)


def _dot_nt(a, b):
    return lax.dot_general(a, b, NT_DIMS, preferred_element_type=F32)


def _softplus_tail(z):
    return jnp.log1p(jnp.exp(-jnp.abs(z)))


def _split_bf16(x):
    hi = x.astype(BF16)
    lo = (x - hi.astype(F32)).astype(BF16)
    return hi, lo


def _norm_matmul_kernel(x_ref, g_ref, w_ref, o_ref, xn_ref):
    @pl.when(pl.program_id(1) == 0)
    def _():
        xn_ref[...] = _rms(x_ref[...], g_ref[...]).astype(BF16)

    o_ref[...] = _dot(xn_ref[...], w_ref[...]).astype(o_ref.dtype)


def _norm_matmul(x, g, w, tm, tn):
    m, k = x.shape
    n = w.shape[1]
    return pl.pallas_call(
        _norm_matmul_kernel,
        grid=(m // tm, n // tn),
        in_specs=[
            pl.BlockSpec((tm, k), lambda i, j: (i, 0)),
            pl.BlockSpec((1, k), lambda i, j: (0, 0)),
            pl.BlockSpec((k, tn), lambda i, j: (0, j)),
        ],
        out_specs=pl.BlockSpec((tm, tn), lambda i, j: (i, j)),
        out_shape=jax.ShapeDtypeStruct((m, n), BF16),
        scratch_shapes=[pltpu.VMEM((tm, k), BF16)],
        compiler_params=_params("parallel", "arbitrary"),
        name="norm_matmul",
    )(x, g.reshape(1, k), w)


def _matmul_kernel(x_ref, w_ref, o_ref):
    o_ref[...] = _dot(x_ref[...], w_ref[...]).astype(o_ref.dtype)


def _matmul(xn, w, tm, tn):
    m, k = xn.shape
    n = w.shape[1]
    return pl.pallas_call(
        _matmul_kernel,
        grid=(m // tm, n // tn),
        in_specs=[
            pl.BlockSpec((tm, k), lambda i, j: (i, 0)),
            pl.BlockSpec((k, tn), lambda i, j: (0, j)),
        ],
        out_specs=pl.BlockSpec((tm, tn), lambda i, j: (i, j)),
        out_shape=jax.ShapeDtypeStruct((m, n), BF16),
        compiler_params=_params("parallel", "arbitrary"),
        name="matmul",
    )(xn, w)


def _proj_residual_kernel(*refs, n_in):
    h_ref, g_ref, w_ref = refs[:3]
    a_refs = refs[3:3 + n_in]
    o_ref, xn_ref = refs[3 + n_in:]
    acc = h_ref[...]
    row = 0
    for a_ref in a_refs:
        k = a_ref.shape[1]
        acc = acc + _dot(a_ref[...], w_ref[row:row + k, :])
        row += k
    o_ref[...] = acc
    xn_ref[...] = _rms(acc, g_ref[...]).astype(BF16)


def _resident(shape):
    return pl.BlockSpec(shape, lambda i: (0,) * len(shape), pipeline_mode=pl.Buffered(1))


def _proj_residual(h, g, a_list, w, tm):
    m, n = h.shape
    assert sum(a.shape[1] for a in a_list) == w.shape[0]
    row_block = pl.BlockSpec((tm, n), lambda i: (i, 0))
    in_specs = [row_block, _resident((1, n)), _resident(w.shape)]
    in_specs += [pl.BlockSpec((tm, a.shape[1]), lambda i: (i, 0)) for a in a_list]
    return pl.pallas_call(
        functools.partial(_proj_residual_kernel, n_in=len(a_list)),
        grid=(m // tm,),
        in_specs=in_specs,
        out_specs=[row_block, row_block],
        out_shape=[jax.ShapeDtypeStruct((m, n), F32), jax.ShapeDtypeStruct((m, n), BF16)],
        compiler_params=_params("parallel"),
        name="proj_residual",
    )(h, g.reshape(1, n), w, *a_list)


def _ffn_kernel(xn_ref, wg_ref, wu_ref, wd_ref, o_ref):
    @pl.when(pl.program_id(1) == 0)
    def _():
        o_ref[...] = jnp.zeros(o_ref.shape, F32)

    xn = xn_ref[...]
    gate = _dot(xn, wg_ref[...])
    up = _dot(xn, wu_ref[...])
    act = (gate * jax.nn.sigmoid(gate) * up).astype(BF16)
    o_ref[...] += _dot(act, wd_ref[...])


def _ffn(xn, wg, wu, wd, tm, th):
    m, d = xn.shape
    hid = wg.shape[1]
    return pl.pallas_call(
        _ffn_kernel,
        grid=(m // tm, hid // th),
        in_specs=[
            pl.BlockSpec((tm, d), lambda i, j: (i, 0)),
            pl.BlockSpec((d, th), lambda i, j: (0, j)),
            pl.BlockSpec((d, th), lambda i, j: (0, j)),
            pl.BlockSpec((th, d), lambda i, j: (j, 0)),
        ],
        out_specs=pl.BlockSpec((tm, d), lambda i, j: (i, 0)),
        out_shape=jax.ShapeDtypeStruct((m, d), F32),
        compiler_params=_params("parallel", "arbitrary"),
        name="ffn",
    )(xn, wg, wu, wd)


def _ple_kernel(x_ref, y_ref, g_ref, wg_ref, p_ref, wp_ref, ng_ref, *o_refs, last):
    x = x_ref[...] + y_ref[...]
    gate = jax.nn.sigmoid(_dot(_rms(x, g_ref[...]).astype(BF16), wg_ref[...]))
    out = x + _dot(p_ref[...].astype(BF16), wp_ref[...]) * gate
    if last:
        o_refs[0][...] = _rms(out, ng_ref[...])
    else:
        o_refs[0][...] = out
        o_refs[1][...] = _rms(out, ng_ref[...]).astype(BF16)


def _ple(h, y, g, wg, p, layer, wp, next_g, tm, last):
    m, d = h.shape
    pd = p.shape[2]
    row_block = pl.BlockSpec((tm, d), lambda i: (i, 0))
    h_shape = jax.ShapeDtypeStruct((m, d), F32)
    return pl.pallas_call(
        functools.partial(_ple_kernel, last=last),
        grid=(m // tm,),
        in_specs=[
            row_block,
            row_block,
            _resident((1, d)),
            _resident(wg.shape),
            pl.BlockSpec((None, tm, pd), lambda i: (layer, i, 0)),
            _resident(wp.shape),
            _resident((1, d)),
        ],
        out_specs=row_block if last else [row_block, row_block],
        out_shape=h_shape if last else [h_shape, jax.ShapeDtypeStruct((m, d), BF16)],
        compiler_params=_params("parallel"),
        name="ple",
    )(h, y, g.reshape(1, d), wg, p, wp, next_g.reshape(1, d))


def _key_to_float(key):
    bits = key ^ ((key >> 31) & 0x7FFFFFFF)
    return lax.bitcast_convert_type(bits, F32)


def _dsa_kernel(qi_ref, qsm_ref, ksm_ref, qa_ref, c_ref, o_ref,
                sc_ref, ct_ref, acc_ref, m_ref, l_ref, *, topk, qb_size, seq):
    t = qb_size
    qb = pl.program_id(1)
    nk = qb + 1

    @pl.when(qb == 0)
    def _():
        ct_ref[...] = jnp.transpose(c_ref[...].astype(F32)).astype(BF16)

    row = lax.broadcasted_iota(I32, (t, t), 0)
    col = lax.broadcasted_iota(I32, (t, t), 1)
    limit = (((qb * t + col) >> CHUNK_SHIFT) + 1) << CHUNK_SHIFT

    w_t = jnp.transpose(qsm_ref[...].astype(F32))[SM_W:SM_W + IDX_HEADS, :]

    def score_tile(kt, carry):
        rows = pl.ds(pl.multiple_of(kt * t, t), t)
        kk = ksm_ref[rows, :][:, :IDX_DIM]
        sc = jnp.zeros((t, t), F32)
        for h in range(IDX_HEADS):
            qh = qi_ref[:, h * IDX_DIM:(h + 1) * IDX_DIM]
            sc = sc + jnp.maximum(_dot_nt(kk, qh), 0.0) * w_t[h:h + 1, :]
        sc = sc * (IDX_DIM ** -0.5 * IDX_HEADS ** -0.5)
        sc_ref[rows, :] = jnp.where(kt * t + row < limit, sc, -jnp.inf)
        return carry

    lax.fori_loop(0, nk, score_tile, 0)

    def count(pred):
        def body(kt, acc):
            rows = pl.ds(pl.multiple_of(kt * t, t), t)
            hit = pred(sc_ref[rows, :]).astype(I32)
            return acc + jnp.sum(hit.reshape(t // SUBLANES, SUBLANES, t), axis=0)
        acc = lax.fori_loop(0, nk, body, jnp.zeros((SUBLANES, t), I32))
        return jnp.sum(acc, axis=0, keepdims=True)

    def count_ge(key):
        cand = _key_to_float(key)
        return count(lambda s: s >= cand)

    zero_key = jnp.zeros((1, t), I32)
    res0 = jnp.where(count_ge(zero_key) >= topk, 0, INT_MIN).astype(I32)

    def bisect(i, res):
        cand = res | jnp.left_shift(jnp.int32(1), 30 - i)
        return jnp.where(count_ge(cand) >= topk, cand, res)

    res = lax.fori_loop(0, 31, bisect, res0)
    thr = _key_to_float(jnp.maximum(res, KEY_NEG_INF))
    n_ge = count(lambda s: s >= thr) + jnp.where(thr == -jnp.inf, seq - nk * t, 0)

    @pl.when(jnp.max(n_ge) > topk)
    def _():
        need = (topk - count(lambda s: s > thr)).astype(F32)
        tri = (row >= col).astype(BF16)

        def body(kt, seen):
            rows = pl.ds(pl.multiple_of(kt * t, t), t)
            s = sc_ref[rows, :]
            eq = s == thr
            rank = _dot(tri, eq.astype(BF16)) + seen
            drop = (eq & (rank > need)) | (kt * t + row >= limit)
            sc_ref[rows, :] = jnp.where(drop, jnp.nan, s)
            return seen + jnp.sum(eq.astype(F32), axis=0, keepdims=True)

        lax.fori_loop(0, nk, body, jnp.zeros((1, t), F32))

    m_ref[...] = jnp.full(m_ref.shape, -1e30, F32)
    l_ref[...] = jnp.zeros(l_ref.shape, F32)
    acc_ref[...] = jnp.zeros(acc_ref.shape, F32)

    heads = range(A_HEADS)
    hsl = [slice(h * A_LATENT, (h + 1) * A_LATENT) for h in heads]
    c2 = (A_LATENT ** -0.5) * LOG2_E

    def attend(kt, carry):
        rows = pl.ds(pl.multiple_of(kt * t, t), t)
        valid = sc_ref[rows, :] >= thr
        ck = c_ref[rows, :]
        ckt = ct_ref[:, rows]
        raws = [jnp.where(valid, _dot_nt(ck, qa_ref[:, hs]), -1e30) for hs in hsl]
        ps, alphas = [], []
        for h, raw in zip(heads, raws):
            m_old = m_ref[h:h + 1, :]
            m_new = jnp.maximum(m_old, jnp.max(raw, axis=0, keepdims=True))
            alpha = jnp.exp2((m_old - m_new) * c2)
            p = jnp.exp2((raw - m_new) * c2)
            l_ref[h:h + 1, :] = alpha * l_ref[h:h + 1, :] + jnp.sum(p, axis=0, keepdims=True)
            m_ref[h:h + 1, :] = m_new
            ps.append(p.astype(BF16))
            alphas.append(alpha)
        pvs = [_dot(ckt, p) for p in ps]
        for hs, alpha, pv in zip(hsl, alphas, pvs):
            acc_ref[hs, :] = alpha * acc_ref[hs, :] + pv
        return carry

    lax.fori_loop(0, nk, attend, 0)

    for h in heads:
        out = acc_ref[hsl[h], :] / l_ref[h:h + 1, :]
        o_ref[:, hsl[h]] = jnp.transpose(out).astype(o_ref.dtype)


def _dsa(u, batch, seq, qb_size):
    m = batch * seq
    nqb = seq // qb_size
    topk = min(A_TOPK_MAX, seq // 4)
    wide = A_HEADS * A_LATENT
    return pl.pallas_call(
        functools.partial(_dsa_kernel, topk=topk, qb_size=qb_size, seq=seq),
        grid=(batch, nqb),
        in_specs=[
            pl.BlockSpec((qb_size, wide), lambda b, q: (b * nqb + q, U_QI // wide)),
            pl.BlockSpec((qb_size, LANES), lambda b, q: (b * nqb + q, U_SMALL // LANES)),
            pl.BlockSpec((seq, LANES), lambda b, q: (b, U_SMALL // LANES)),
            pl.BlockSpec((qb_size, wide), lambda b, q: (b * nqb + q, U_QA // wide)),
            pl.BlockSpec((seq, LANES), lambda b, q: (b, U_CA // LANES)),
        ],
        out_specs=pl.BlockSpec((qb_size, wide), lambda b, q: (b * nqb + q, 0)),
        out_shape=jax.ShapeDtypeStruct((m, wide), BF16),
        scratch_shapes=[
            pltpu.VMEM((seq, qb_size), F32),
            pltpu.VMEM((A_LATENT, seq), BF16),
            pltpu.VMEM((wide, qb_size), F32),
            pltpu.VMEM((A_HEADS, qb_size), F32),
            pltpu.VMEM((A_HEADS, qb_size), F32),
        ],
        compiler_params=_params("parallel", "arbitrary"),
        name="dsa",
    )(u, u, u, u, u)


def _gla_kernel(q_ref, k_ref, v_ref, g_ref, sm_ref, wgk_ref, bgk_ref, gn_ref, o_ref, st_ref,
                *, nchunks):
    c = CHUNK
    n = GLA_CHUNKS_PER_STEP
    span = n * c
    row = lax.broadcasted_iota(I32, (span, span), 0)
    col = lax.broadcasted_iota(I32, (span, span), 1)
    tri = ((row >= col) & ((row >> CHUNK_SHIFT) == (col >> CHUNK_SHIFT))).astype(BF16)
    causal = lax.broadcasted_iota(I32, (c, c), 0) >= lax.broadcasted_iota(I32, (c, c), 1)
    st_ref[...] = jnp.zeros(st_ref.shape, F32)
    wgk = wgk_ref[...]
    bgk = bgk_ref[...]
    gn = gn_ref[...]
    heads = range(B_HEADS)
    ksl = [slice(h * B_DK, (h + 1) * B_DK) for h in heads]
    vsl = [slice(h * B_DV, (h + 1) * B_DV) for h in heads]
    work = [(s, h) for s in range(n) for h in heads]

    def step(si, carry):
        base = pl.multiple_of(si * span, span)
        rows = pl.ds(base, span)
        x = _dot(sm_ref[rows, :], wgk) + bgk
        gk = -(jnp.maximum(-x, 0.0) + _softplus_tail(x)) / B_GATE_NORM
        hi, lo = _split_bf16(gk)
        cum_all = _dot(tri, hi) + _dot(tri, lo)
        q_es, k_es, k_ds, decays, vs_ = {}, {}, {}, {}, {}
        for s, h in work:
            sub = pl.ds(base + s * c, c)
            cum = cum_all[s * c:(s + 1) * c, ksl[h]]
            last = cum[c - 1:c, :]
            q = q_ref[sub, ksl[h]].astype(F32) * (B_DK ** -0.5)
            k = k_ref[sub, ksl[h]].astype(F32)
            q_es[s, h] = (q * jnp.exp(cum)).astype(BF16)
            k_es[s, h] = (k * jnp.exp(-cum)).astype(BF16)
            k_ds[s, h] = (k * jnp.exp(last - cum)).astype(BF16)
            decays[s, h] = jnp.exp(last)
            vs_[s, h] = v_ref[sub, vsl[h]]
        atts = {w: _dot_nt(q_es[w], k_es[w]) for w in work}
        kvs = {w: lax.dot_general(vs_[w], k_ds[w], TN_DIMS, preferred_element_type=F32)
               for w in work}
        sts = {h: st_ref[h] for h in heads}
        inter = {}
        for s in range(n):
            for h in heads:
                inter[s, h] = _dot_nt(q_es[s, h], sts[h].astype(BF16))
            for h in heads:
                sts[h] = sts[h] * decays[s, h] + kvs[s, h]
        for h in heads:
            st_ref[h] = sts[h]
        atts = {w: jnp.where(causal, atts[w], 0.0).astype(BF16) for w in work}
        intra = {w: _dot(atts[w], vs_[w]) for w in work}
        for s, h in work:
            sub = pl.ds(base + s * c, c)
            gate = g_ref[sub, vsl[h]].astype(F32)
            o = intra[s, h] + inter[s, h]
            o_ref[sub, vsl[h]] = (_rms(o, gn) * (gate * jax.nn.sigmoid(gate))).astype(o_ref.dtype)
        return carry

    lax.fori_loop(0, nchunks // n, step, 0)


def _gla(u, wgk_pad, bgk, gnorm, batch, seq):
    m = batch * seq
    qk_wide = B_HEADS * B_DK
    v_wide = B_HEADS * B_DV
    return pl.pallas_call(
        functools.partial(_gla_kernel, nchunks=seq // CHUNK),
        grid=(batch,),
        in_specs=[
            pl.BlockSpec((seq, qk_wide), lambda b: (b, U_QB // qk_wide)),
            pl.BlockSpec((seq, qk_wide), lambda b: (b, U_KB // qk_wide)),
            pl.BlockSpec((seq, v_wide), lambda b: (b, U_VB // v_wide)),
            pl.BlockSpec((seq, v_wide), lambda b: (b, U_GB // v_wide)),
            pl.BlockSpec((seq, LANES), lambda b: (b, U_SMALL // LANES)),
            pl.BlockSpec((LANES, qk_wide), lambda b: (0, 0)),
            pl.BlockSpec((1, qk_wide), lambda b: (0, 0)),
            pl.BlockSpec((1, B_DV), lambda b: (0, 0)),
        ],
        out_specs=pl.BlockSpec((seq, v_wide), lambda b: (b, 0)),
        out_shape=jax.ShapeDtypeStruct((m, v_wide), BF16),
        scratch_shapes=[pltpu.VMEM((B_HEADS, B_DV, B_DK), F32)],
        compiler_params=_params("parallel"),
        name="gla",
    )(u, u, u, u, u, wgk_pad, bgk.reshape(1, -1), gnorm.reshape(1, -1))


def _sb_kernel(q_ref, k_ref, v_ref, o_ref, *, blk, nblk, head_dim, heads):
    t = blk
    d = head_dim
    row = lax.broadcasted_iota(I32, (t, t), 0)
    col = lax.broadcasted_iota(I32, (t, t), 1)
    upper = (row > col).astype(BF16)
    diag = col < row
    scale = d ** -0.5

    hsl = [slice(h * d, (h + 1) * d) for h in range(heads)]

    def block(qrows, krows, runs, mask):
        zs = [_dot_nt(q_ref[qrows, hs], k_ref[krows, hs]) * scale for hs in hsl]
        lms, log_bs = [], []
        for z in zs:
            log_1mb = -(jnp.maximum(z, 0.0) + jnp.log(1.0 + jnp.exp(-jnp.abs(z))))
            log_bs.append(log_1mb + z)
            lms.append(log_1mb if mask is None else jnp.where(mask, log_1mb, 0.0))
        sufs = [_dot(lm.astype(BF16), upper) for lm in lms]
        ws = []
        for suf, log_b, run in zip(sufs, log_bs, runs):
            w = jnp.exp(suf + run + log_b)
            ws.append((w if mask is None else jnp.where(mask, w, 0.0)).astype(BF16))
        outs = [_dot(w, v_ref[krows, hs]) for w, hs in zip(ws, hsl)]
        new_runs = [run + jnp.sum(lm, axis=1, keepdims=True) for run, lm in zip(runs, lms)]
        return new_runs, outs

    def q_block(qb, carry):
        qrows = pl.ds(pl.multiple_of(qb * t, t), t)
        runs, accs = block(qrows, qrows, [jnp.zeros((t, 1), F32)] * heads, diag)

        def alive(rs):
            top = jnp.max(rs[0])
            for r in rs[1:]:
                top = jnp.maximum(top, jnp.max(r))
            return (top >= SB_EXP_UNDERFLOW).astype(I32)

        def cond(state):
            i, live, _, _ = state
            return jnp.logical_and(i <= qb, live > 0)

        def k_block(state):
            i, _, rs, acs = state
            krows = pl.ds(pl.multiple_of((qb - i) * t, t), t)
            new_rs, outs = block(qrows, krows, list(rs), None)
            new_acs = [a + o for a, o in zip(acs, outs)]
            return i + 1, alive(new_rs), tuple(new_rs), tuple(new_acs)

        state = (jnp.int32(1), alive(runs), tuple(runs), tuple(accs))
        _, _, _, accs = lax.while_loop(cond, k_block, state)
        for h in range(heads):
            o_ref[qrows, h * d:(h + 1) * d] = accs[h].astype(o_ref.dtype)
        return carry

    lax.fori_loop(0, nblk, q_block, 0)


def _stick_breaking(qkv, batch, seq, blk):
    m = batch * seq
    d = qkv.shape[1] // (3 * C_HEADS)
    groups = C_HEADS // SB_HEADS_PER_STEP
    wide = SB_HEADS_PER_STEP * d
    return pl.pallas_call(
        functools.partial(_sb_kernel, blk=blk, nblk=seq // blk, head_dim=d, heads=SB_HEADS_PER_STEP),
        grid=(batch, groups),
        in_specs=[
            pl.BlockSpec((seq, wide), lambda b, g: (b, g)),
            pl.BlockSpec((seq, wide), lambda b, g: (b, groups + g)),
            pl.BlockSpec((seq, wide), lambda b, g: (b, 2 * groups + g)),
        ],
        out_specs=pl.BlockSpec((seq, wide), lambda b, g: (b, g)),
        out_shape=jax.ShapeDtypeStruct((m, C_HEADS * d), BF16),
        compiler_params=_params("parallel", "arbitrary"),
        name="stick_breaking",
    )(qkv, qkv, qkv)


EVEN_SRC_COLS = ((0, 1024), (1152, 2176), (3280, 4304), (4304, 5328), (2256, 2768),
                 (2768, 3280), (1024, 1152), (2176, 2240), (2240, 2256), (5328, 5344))


def _permute_w_in_kernel(w_ref, o_ref):
    w = w_ref[...]
    parts = [w[:, a:b] for a, b in EVEN_SRC_COLS]
    used = sum(b - a for a, b in EVEN_SRC_COLS)
    parts.append(jnp.zeros((w.shape[0], U_WIDTH - used), w.dtype))
    o_ref[...] = jnp.concatenate(parts, axis=1).astype(o_ref.dtype)


def _permute_even_w_in(w, rows):
    d, n = w.shape
    return pl.pallas_call(
        _permute_w_in_kernel,
        grid=(d // rows,),
        in_specs=[pl.BlockSpec((rows, n), lambda i: (i, 0))],
        out_specs=pl.BlockSpec((rows, U_WIDTH), lambda i: (i, 0)),
        out_shape=jax.ShapeDtypeStruct((d, U_WIDTH), BF16),
        compiler_params=_params("parallel"),
        name="permute_w_in",
    )(w)


def kernel(x, p, even_norm, even_w_in, even_w_gk, even_b_gk, even_gla_norm, even_w_out,
           odd_norm, odd_w_in, odd_w_out, ffn_norm, ffn_w_gate, ffn_w_up, ffn_w_down,
           ple_norm, ple_w_gate, ple_w_proj, final_norm):
    batch, seq, d = x.shape
    depth = p.shape[0]
    m = batch * seq
    h = x.reshape(m, d)
    tm = _tile(m, 1024)
    tm_row = _tile(m, 512)
    tn_pref = 1792
    blk = _tile(seq, 256)

    def mixer_norm(i):
        return even_norm[i // 2] if i % 2 == 0 else odd_norm[i // 2]

    def in_proj(h, hn, g, w, tn):
        return _norm_matmul(h, g, w, tm, tn) if hn is None else _matmul(hn, w, tm, tn)

    hn = None
    for i in range(depth):
        j = i // 2
        if i % 2 == 0:
            w_in = _permute_even_w_in(even_w_in[j], _tile(d, 256))
            u = in_proj(h, hn, even_norm[j], w_in, _tile(U_WIDTH, tn_pref))
            o_a = _dsa(u, batch, seq, blk)
            wgk_pad = jnp.zeros((LANES, B_HEADS * B_DK), F32)
            wgk_pad = wgk_pad.at[SM_GK:SM_GK + B_GATE_RANK].set(even_w_gk[j]).astype(BF16)
            o_b = _gla(u, wgk_pad, even_b_gk[j], even_gla_norm[j], batch, seq)
            h, xn = _proj_residual(h, ffn_norm[i], [o_a, o_b], even_w_out[j].astype(BF16), tm_row)
        else:
            qkv = in_proj(h, hn, odd_norm[j], odd_w_in[j].astype(BF16), _tile(3 * d, tn_pref))
            o = _stick_breaking(qkv, batch, seq, blk)
            h, xn = _proj_residual(h, ffn_norm[i], [o], odd_w_out[j].astype(BF16), tm_row)
        y = _ffn(xn, ffn_w_gate[i].astype(BF16), ffn_w_up[i].astype(BF16),
                 ffn_w_down[i].astype(BF16), tm, _tile(ffn_w_gate.shape[2], 512))
        last = i == depth - 1
        out = _ple(h, y, ple_norm[i], ple_w_gate[i].astype(BF16), p.reshape(depth, m, -1), i,
                   ple_w_proj[i].astype(BF16), final_norm if last else mixer_norm(i + 1),
                   tm_row, last)
        h, hn = (out, None) if last else out
    return h.reshape(batch, seq, d)
```

```python
import functools

import jax
import jax.numpy as jnp
from jax import lax
from jax.experimental import pallas as pl
from jax.experimental.pallas import tpu as pltpu

F32 = jnp.float32
BF16 = jnp.bfloat16
I32 = jnp.int32

EPS = 1e-6
CHUNK = 64
CHUNK_SHIFT = CHUNK.bit_length() - 1
A_HEADS = 8
A_LATENT = 128
A_TOPK_MAX = 256
IDX_HEADS = 16
IDX_DIM = 64
B_HEADS = 4
B_DK = 128
B_DV = 256
B_GATE_RANK = 16
B_GATE_NORM = 16.0
GLA_CHUNKS_PER_STEP = 4
C_HEADS = 16
LOG2_E = 1.4426950408889634
SB_HEADS_PER_STEP = 4
SB_EXP_UNDERFLOW = -104.0

LANES = 128
SUBLANES = 8
VMEM_LIMIT_BYTES = 56 * 1024 * 1024

U_QA = 0
U_QI = 1024
U_VB = 2048
U_GB = 3072
U_QB = 4096
U_KB = 4608
U_CA = 5120
U_SMALL = 5248
U_WIDTH = 5376
SM_W = 64
SM_GK = 80

NT_DIMS = (((1,), (1,)), ((), ()))
TN_DIMS = (((0,), (0,)), ((), ()))

KEY_NEG_INF = -2139095041
INT_MIN = -2147483648


def _params(*sem):
    return pltpu.CompilerParams(dimension_semantics=sem, vmem_limit_bytes=VMEM_LIMIT_BYTES)


def _tile(n, pref):
    if n <= pref:
        return n
    t = (pref // LANES) * LANES
    while n % t:
        t -= LANES
    return t


def _rms(x, g):
    ms = jnp.mean(x * x, axis=-1, keepdims=True)
    return x * lax.rsqrt(ms + EPS) * g


def _dot(a, b):
    return jnp.dot(a, b, preferred_element_type=F32)


def _dot_nt(a, b):
    return lax.dot_general(a, b, NT_DIMS, preferred_element_type=F32)


def _softplus_tail(z):
    return jnp.log1p(jnp.exp(-jnp.abs(z)))


def _split_bf16(x):
    hi = x.astype(BF16)
    lo = (x - hi.astype(F32)).astype(BF16)
    return hi, lo


def _norm_matmul_kernel(x_ref, g_ref, w_ref, o_ref, xn_ref):
    @pl.when(pl.program_id(1) == 0)
    def _():
        xn_ref[...] = _rms(x_ref[...], g_ref[...]).astype(BF16)

    o_ref[...] = _dot(xn_ref[...], w_ref[...]).astype(o_ref.dtype)


def _norm_matmul(x, g, w, tm, tn):
    m, k = x.shape
    n = w.shape[1]
    return pl.pallas_call(
        _norm_matmul_kernel,
        grid=(m // tm, n // tn),
        in_specs=[
            pl.BlockSpec((tm, k), lambda i, j: (i, 0)),
            pl.BlockSpec((1, k), lambda i, j: (0, 0)),
            pl.BlockSpec((k, tn), lambda i, j: (0, j)),
        ],
        out_specs=pl.BlockSpec((tm, tn), lambda i, j: (i, j)),
        out_shape=jax.ShapeDtypeStruct((m, n), BF16),
        scratch_shapes=[pltpu.VMEM((tm, k), BF16)],
        compiler_params=_params("parallel", "arbitrary"),
        name="norm_matmul",
    )(x, g.reshape(1, k), w)


def _matmul_kernel(x_ref, w_ref, o_ref):
    o_ref[...] = _dot(x_ref[...], w_ref[...]).astype(o_ref.dtype)


def _matmul(xn, w, tm, tn):
    m, k = xn.shape
    n = w.shape[1]
    return pl.pallas_call(
        _matmul_kernel,
        grid=(m // tm, n // tn),
        in_specs=[
            pl.BlockSpec((tm, k), lambda i, j: (i, 0)),
            pl.BlockSpec((k, tn), lambda i, j: (0, j)),
        ],
        out_specs=pl.BlockSpec((tm, tn), lambda i, j: (i, j)),
        out_shape=jax.ShapeDtypeStruct((m, n), BF16),
        compiler_params=_params("parallel", "arbitrary"),
        name="matmul",
    )(xn, w)


def _proj_residual_kernel(*refs, n_in):
    h_ref, g_ref, w_ref = refs[:3]
    a_refs = refs[3:3 + n_in]
    o_ref, xn_ref = refs[3 + n_in:]
    acc = h_ref[...]
    row = 0
    for a_ref in a_refs:
        k = a_ref.shape[1]
        acc = acc + _dot(a_ref[...], w_ref[row:row + k, :])
        row += k
    o_ref[...] = acc
    xn_ref[...] = _rms(acc, g_ref[...]).astype(BF16)


def _resident(shape):
    return pl.BlockSpec(shape, lambda i: (0,) * len(shape), pipeline_mode=pl.Buffered(1))


def _proj_residual(h, g, a_list, w, tm):
    m, n = h.shape
    assert sum(a.shape[1] for a in a_list) == w.shape[0]
    row_block = pl.BlockSpec((tm, n), lambda i: (i, 0))
    in_specs = [row_block, _resident((1, n)), _resident(w.shape)]
    in_specs += [pl.BlockSpec((tm, a.shape[1]), lambda i: (i, 0)) for a in a_list]
    return pl.pallas_call(
        functools.partial(_proj_residual_kernel, n_in=len(a_list)),
        grid=(m // tm,),
        in_specs=in_specs,
        out_specs=[row_block, row_block],
        out_shape=[jax.ShapeDtypeStruct((m, n), F32), jax.ShapeDtypeStruct((m, n), BF16)],
        compiler_params=_params("parallel"),
        name="proj_residual",
    )(h, g.reshape(1, n), w, *a_list)


def _ffn_kernel(xn_ref, wg_ref, wu_ref, wd_ref, o_ref):
    @pl.when(pl.program_id(1) == 0)
    def _():
        o_ref[...] = jnp.zeros(o_ref.shape, F32)

    xn = xn_ref[...]
    gate = _dot(xn, wg_ref[...])
    up = _dot(xn, wu_ref[...])
    act = (gate * jax.nn.sigmoid(gate) * up).astype(BF16)
    o_ref[...] += _dot(act, wd_ref[...])


def _ffn(xn, wg, wu, wd, layer, tm, th):
    m, d = xn.shape
    hid = wg.shape[2]
    return pl.pallas_call(
        _ffn_kernel,
        grid=(m // tm, hid // th),
        in_specs=[
            pl.BlockSpec((tm, d), lambda i, j: (i, 0)),
            pl.BlockSpec((None, d, th), lambda i, j: (layer, 0, j)),
            pl.BlockSpec((None, d, th), lambda i, j: (layer, 0, j)),
            pl.BlockSpec((None, th, d), lambda i, j: (layer, j, 0)),
        ],
        out_specs=pl.BlockSpec((tm, d), lambda i, j: (i, 0)),
        out_shape=jax.ShapeDtypeStruct((m, d), F32),
        compiler_params=_params("parallel", "arbitrary"),
        name="ffn",
    )(xn, wg, wu, wd)


def _ple_kernel(x_ref, y_ref, g_ref, wg_ref, p_ref, wp_ref, ng_ref, *o_refs, last):
    x = x_ref[...] + y_ref[...]
    gate = jax.nn.sigmoid(_dot(_rms(x, g_ref[...]).astype(BF16), wg_ref[...]))
    out = x + _dot(p_ref[...].astype(BF16), wp_ref[...]) * gate
    if last:
        o_refs[0][...] = _rms(out, ng_ref[...])
    else:
        o_refs[0][...] = out
        o_refs[1][...] = _rms(out, ng_ref[...]).astype(BF16)


def _ple(h, y, g, wg, p, layer, wp, next_g, tm, last):
    m, d = h.shape
    pd = p.shape[2]
    row_block = pl.BlockSpec((tm, d), lambda i: (i, 0))
    h_shape = jax.ShapeDtypeStruct((m, d), F32)

    def resident_layer(rows, cols):
        return pl.BlockSpec((None, rows, cols), lambda i: (layer, 0, 0),
                            pipeline_mode=pl.Buffered(1))

    return pl.pallas_call(
        functools.partial(_ple_kernel, last=last),
        grid=(m // tm,),
        in_specs=[
            row_block,
            row_block,
            _resident((1, d)),
            resident_layer(d, d),
            pl.BlockSpec((None, tm, pd), lambda i: (layer, i, 0)),
            resident_layer(pd, d),
            _resident((1, d)),
        ],
        out_specs=row_block if last else [row_block, row_block],
        out_shape=h_shape if last else [h_shape, jax.ShapeDtypeStruct((m, d), BF16)],
        compiler_params=_params("parallel"),
        name="ple",
    )(h, y, g.reshape(1, d), wg, p, wp, next_g.reshape(1, d))


def _key_to_float(key):
    bits = key ^ ((key >> 31) & 0x7FFFFFFF)
    return lax.bitcast_convert_type(bits, F32)


def _dsa_kernel(qi_ref, qsm_ref, ksm_ref, qa_ref, c_ref, o_ref,
                sc_ref, ct_ref, acc_ref, m_ref, l_ref, *, topk, qb_size, seq):
    t = qb_size
    qb = pl.program_id(1)
    nk = qb + 1

    @pl.when(qb == 0)
    def _():
        ct_ref[...] = jnp.transpose(c_ref[...].astype(F32)).astype(BF16)

    row = lax.broadcasted_iota(I32, (t, t), 0)
    col = lax.broadcasted_iota(I32, (t, t), 1)
    limit = (((qb * t + col) >> CHUNK_SHIFT) + 1) << CHUNK_SHIFT

    w_t = jnp.transpose(qsm_ref[...].astype(F32))[SM_W:SM_W + IDX_HEADS, :]

    def score_tile(kt, carry):
        rows = pl.ds(pl.multiple_of(kt * t, t), t)
        kk = ksm_ref[rows, :][:, :IDX_DIM]
        sc = jnp.zeros((t, t), F32)
        for h in range(IDX_HEADS):
            qh = qi_ref[:, h * IDX_DIM:(h + 1) * IDX_DIM]
            sc = sc + jnp.maximum(_dot_nt(kk, qh), 0.0) * w_t[h:h + 1, :]
        sc = sc * (IDX_DIM ** -0.5 * IDX_HEADS ** -0.5)
        sc_ref[rows, :] = jnp.where(kt * t + row < limit, sc, -jnp.inf)
        return carry

    lax.fori_loop(0, nk, score_tile, 0)

    def count(pred):
        def body(kt, acc):
            rows = pl.ds(pl.multiple_of(kt * t, t), t)
            hit = pred(sc_ref[rows, :]).astype(I32)
            return acc + jnp.sum(hit.reshape(t // SUBLANES, SUBLANES, t), axis=0)
        acc = lax.fori_loop(0, nk, body, jnp.zeros((SUBLANES, t), I32))
        return jnp.sum(acc, axis=0, keepdims=True)

    def count_ge(key):
        cand = _key_to_float(key)
        return count(lambda s: s >= cand)

    zero_key = jnp.zeros((1, t), I32)
    res0 = jnp.where(count_ge(zero_key) >= topk, 0, INT_MIN).astype(I32)

    def bisect(i, res):
        cand = res | jnp.left_shift(jnp.int32(1), 30 - i)
        return jnp.where(count_ge(cand) >= topk, cand, res)

    res = lax.fori_loop(0, 31, bisect, res0)
    thr = _key_to_float(jnp.maximum(res, KEY_NEG_INF))
    n_ge = count(lambda s: s >= thr) + jnp.where(thr == -jnp.inf, seq - nk * t, 0)

    @pl.when(jnp.max(n_ge) > topk)
    def _():
        need = (topk - count(lambda s: s > thr)).astype(F32)
        tri = (row >= col).astype(BF16)

        def body(kt, seen):
            rows = pl.ds(pl.multiple_of(kt * t, t), t)
            s = sc_ref[rows, :]
            eq = s == thr
            rank = _dot(tri, eq.astype(BF16)) + seen
            drop = (eq & (rank > need)) | (kt * t + row >= limit)
            sc_ref[rows, :] = jnp.where(drop, jnp.nan, s)
            return seen + jnp.sum(eq.astype(F32), axis=0, keepdims=True)

        lax.fori_loop(0, nk, body, jnp.zeros((1, t), F32))

    m_ref[...] = jnp.full(m_ref.shape, -1e30, F32)
    l_ref[...] = jnp.zeros(l_ref.shape, F32)
    acc_ref[...] = jnp.zeros(acc_ref.shape, F32)

    heads = range(A_HEADS)
    hsl = [slice(h * A_LATENT, (h + 1) * A_LATENT) for h in heads]
    c2 = (A_LATENT ** -0.5) * LOG2_E

    def attend(kt, carry):
        rows = pl.ds(pl.multiple_of(kt * t, t), t)
        valid = sc_ref[rows, :] >= thr
        ck = c_ref[rows, :]
        ckt = ct_ref[:, rows]
        raws = [jnp.where(valid, _dot_nt(ck, qa_ref[:, hs]), -1e30) for hs in hsl]
        ps, alphas = [], []
        for h, raw in zip(heads, raws):
            m_old = m_ref[h:h + 1, :]
            m_new = jnp.maximum(m_old, jnp.max(raw, axis=0, keepdims=True))
            alpha = jnp.exp2((m_old - m_new) * c2)
            p = jnp.exp2((raw - m_new) * c2)
            l_ref[h:h + 1, :] = alpha * l_ref[h:h + 1, :] + jnp.sum(p, axis=0, keepdims=True)
            m_ref[h:h + 1, :] = m_new
            ps.append(p.astype(BF16))
            alphas.append(alpha)
        pvs = [_dot(ckt, p) for p in ps]
        for hs, alpha, pv in zip(hsl, alphas, pvs):
            acc_ref[hs, :] = alpha * acc_ref[hs, :] + pv
        return carry

    lax.fori_loop(0, nk, attend, 0)

    for h in heads:
        out = acc_ref[hsl[h], :] / l_ref[h:h + 1, :]
        o_ref[:, hsl[h]] = jnp.transpose(out).astype(o_ref.dtype)


def _dsa(u, batch, seq, qb_size):
    m = batch * seq
    nqb = seq // qb_size
    topk = min(A_TOPK_MAX, seq // 4)
    wide = A_HEADS * A_LATENT
    return pl.pallas_call(
        functools.partial(_dsa_kernel, topk=topk, qb_size=qb_size, seq=seq),
        grid=(batch, nqb),
        in_specs=[
            pl.BlockSpec((qb_size, wide), lambda b, q: (b * nqb + q, U_QI // wide)),
            pl.BlockSpec((qb_size, LANES), lambda b, q: (b * nqb + q, U_SMALL // LANES)),
            pl.BlockSpec((seq, LANES), lambda b, q: (b, U_SMALL // LANES)),
            pl.BlockSpec((qb_size, wide), lambda b, q: (b * nqb + q, U_QA // wide)),
            pl.BlockSpec((seq, LANES), lambda b, q: (b, U_CA // LANES)),
        ],
        out_specs=pl.BlockSpec((qb_size, wide), lambda b, q: (b * nqb + q, 0)),
        out_shape=jax.ShapeDtypeStruct((m, wide), BF16),
        scratch_shapes=[
            pltpu.VMEM((seq, qb_size), F32),
            pltpu.VMEM((A_LATENT, seq), BF16),
            pltpu.VMEM((wide, qb_size), F32),
            pltpu.VMEM((A_HEADS, qb_size), F32),
            pltpu.VMEM((A_HEADS, qb_size), F32),
        ],
        compiler_params=_params("parallel", "arbitrary"),
        name="dsa",
    )(u, u, u, u, u)


def _gla_kernel(q_ref, k_ref, v_ref, g_ref, sm_ref, wgk_ref, bgk_ref, gn_ref, o_ref, st_ref,
                *, nchunks):
    c = CHUNK
    n = GLA_CHUNKS_PER_STEP
    span = n * c
    row = lax.broadcasted_iota(I32, (span, span), 0)
    col = lax.broadcasted_iota(I32, (span, span), 1)
    tri = ((row >= col) & ((row >> CHUNK_SHIFT) == (col >> CHUNK_SHIFT))).astype(BF16)
    causal = lax.broadcasted_iota(I32, (c, c), 0) >= lax.broadcasted_iota(I32, (c, c), 1)
    st_ref[...] = jnp.zeros(st_ref.shape, F32)
    wgk = wgk_ref[...]
    bgk = bgk_ref[...]
    gn = gn_ref[...]
    heads = range(B_HEADS)
    ksl = [slice(h * B_DK, (h + 1) * B_DK) for h in heads]
    vsl = [slice(h * B_DV, (h + 1) * B_DV) for h in heads]
    work = [(s, h) for s in range(n) for h in heads]

    def step(si, carry):
        base = pl.multiple_of(si * span, span)
        rows = pl.ds(base, span)
        x = _dot(sm_ref[rows, :], wgk) + bgk
        gk = -(jnp.maximum(-x, 0.0) + _softplus_tail(x)) / B_GATE_NORM
        hi, lo = _split_bf16(gk)
        cum_all = _dot(tri, hi) + _dot(tri, lo)
        q_es, k_es, k_ds, decays, vs_ = {}, {}, {}, {}, {}
        for s, h in work:
            sub = pl.ds(base + s * c, c)
            cum = cum_all[s * c:(s + 1) * c, ksl[h]]
            last = cum[c - 1:c, :]
            q = q_ref[sub, ksl[h]].astype(F32) * (B_DK ** -0.5)
            k = k_ref[sub, ksl[h]].astype(F32)
            q_es[s, h] = (q * jnp.exp(cum)).astype(BF16)
            k_es[s, h] = (k * jnp.exp(-cum)).astype(BF16)
            k_ds[s, h] = (k * jnp.exp(last - cum)).astype(BF16)
            decays[s, h] = jnp.exp(last)
            vs_[s, h] = v_ref[sub, vsl[h]]
        atts = {w: _dot_nt(q_es[w], k_es[w]) for w in work}
        kvs = {w: lax.dot_general(vs_[w], k_ds[w], TN_DIMS, preferred_element_type=F32)
               for w in work}
        sts = {h: st_ref[h] for h in heads}
        inter = {}
        for s in range(n):
            for h in heads:
                inter[s, h] = _dot_nt(q_es[s, h], sts[h].astype(BF16))
            for h in heads:
                sts[h] = sts[h] * decays[s, h] + kvs[s, h]
        for h in heads:
            st_ref[h] = sts[h]
        atts = {w: jnp.where(causal, atts[w], 0.0).astype(BF16) for w in work}
        intra = {w: _dot(atts[w], vs_[w]) for w in work}
        for s, h in work:
            sub = pl.ds(base + s * c, c)
            gate = g_ref[sub, vsl[h]].astype(F32)
            o = intra[s, h] + inter[s, h]
            o_ref[sub, vsl[h]] = (_rms(o, gn) * (gate * jax.nn.sigmoid(gate))).astype(o_ref.dtype)
        return carry

    lax.fori_loop(0, nchunks // n, step, 0)


def _gla(u, wgk_pad, bgk, gnorm, batch, seq):
    m = batch * seq
    qk_wide = B_HEADS * B_DK
    v_wide = B_HEADS * B_DV
    return pl.pallas_call(
        functools.partial(_gla_kernel, nchunks=seq // CHUNK),
        grid=(batch,),
        in_specs=[
            pl.BlockSpec((seq, qk_wide), lambda b: (b, U_QB // qk_wide)),
            pl.BlockSpec((seq, qk_wide), lambda b: (b, U_KB // qk_wide)),
            pl.BlockSpec((seq, v_wide), lambda b: (b, U_VB // v_wide)),
            pl.BlockSpec((seq, v_wide), lambda b: (b, U_GB // v_wide)),
            pl.BlockSpec((seq, LANES), lambda b: (b, U_SMALL // LANES)),
            pl.BlockSpec((LANES, qk_wide), lambda b: (0, 0)),
            pl.BlockSpec((1, qk_wide), lambda b: (0, 0)),
            pl.BlockSpec((1, B_DV), lambda b: (0, 0)),
        ],
        out_specs=pl.BlockSpec((seq, v_wide), lambda b: (b, 0)),
        out_shape=jax.ShapeDtypeStruct((m, v_wide), BF16),
        scratch_shapes=[pltpu.VMEM((B_HEADS, B_DV, B_DK), F32)],
        compiler_params=_params("parallel"),
        name="gla",
    )(u, u, u, u, u, wgk_pad, bgk.reshape(1, -1), gnorm.reshape(1, -1))


def _sb_kernel(q_ref, k_ref, v_ref, o_ref, *, blk, nblk, head_dim, heads):
    t = blk
    d = head_dim
    row = lax.broadcasted_iota(I32, (t, t), 0)
    col = lax.broadcasted_iota(I32, (t, t), 1)
    upper = (row > col).astype(BF16)
    diag = col < row
    scale = d ** -0.5

    hsl = [slice(h * d, (h + 1) * d) for h in range(heads)]

    def block(qrows, krows, runs, mask):
        zs = [_dot_nt(q_ref[qrows, hs], k_ref[krows, hs]) * scale for hs in hsl]
        lms, log_bs = [], []
        for z in zs:
            log_1mb = -(jnp.maximum(z, 0.0) + jnp.log(1.0 + jnp.exp(-jnp.abs(z))))
            log_bs.append(log_1mb + z)
            lms.append(log_1mb if mask is None else jnp.where(mask, log_1mb, 0.0))
        sufs = [_dot(lm.astype(BF16), upper) for lm in lms]
        ws = []
        for suf, log_b, run in zip(sufs, log_bs, runs):
            w = jnp.exp(suf + run + log_b)
            ws.append((w if mask is None else jnp.where(mask, w, 0.0)).astype(BF16))
        outs = [_dot(w, v_ref[krows, hs]) for w, hs in zip(ws, hsl)]
        new_runs = [run + jnp.sum(lm, axis=1, keepdims=True) for run, lm in zip(runs, lms)]
        return new_runs, outs

    def q_block(qb, carry):
        qrows = pl.ds(pl.multiple_of(qb * t, t), t)
        runs, accs = block(qrows, qrows, [jnp.zeros((t, 1), F32)] * heads, diag)

        def alive(rs):
            top = jnp.max(rs[0])
            for r in rs[1:]:
                top = jnp.maximum(top, jnp.max(r))
            return (top >= SB_EXP_UNDERFLOW).astype(I32)

        def cond(state):
            i, live, _, _ = state
            return jnp.logical_and(i <= qb, live > 0)

        def k_block(state):
            i, _, rs, acs = state
            krows = pl.ds(pl.multiple_of((qb - i) * t, t), t)
            new_rs, outs = block(qrows, krows, list(rs), None)
            new_acs = [a + o for a, o in zip(acs, outs)]
            return i + 1, alive(new_rs), tuple(new_rs), tuple(new_acs)

        state = (jnp.int32(1), alive(runs), tuple(runs), tuple(accs))
        _, _, _, accs = lax.while_loop(cond, k_block, state)
        for h in range(heads):
            o_ref[qrows, h * d:(h + 1) * d] = accs[h].astype(o_ref.dtype)
        return carry

    lax.fori_loop(0, nblk, q_block, 0)


def _stick_breaking(qkv, batch, seq, blk):
    m = batch * seq
    d = qkv.shape[1] // (3 * C_HEADS)
    groups = C_HEADS // SB_HEADS_PER_STEP
    wide = SB_HEADS_PER_STEP * d
    return pl.pallas_call(
        functools.partial(_sb_kernel, blk=blk, nblk=seq // blk, head_dim=d, heads=SB_HEADS_PER_STEP),
        grid=(batch, groups),
        in_specs=[
            pl.BlockSpec((seq, wide), lambda b, g: (b, g)),
            pl.BlockSpec((seq, wide), lambda b, g: (b, groups + g)),
            pl.BlockSpec((seq, wide), lambda b, g: (b, 2 * groups + g)),
        ],
        out_specs=pl.BlockSpec((seq, wide), lambda b, g: (b, g)),
        out_shape=jax.ShapeDtypeStruct((m, C_HEADS * d), BF16),
        compiler_params=_params("parallel", "arbitrary"),
        name="stick_breaking",
    )(qkv, qkv, qkv)


EVEN_SRC_COLS = ((0, 1024), (1152, 2176), (3280, 4304), (4304, 5328), (2256, 2768),
                 (2768, 3280), (1024, 1152), (2176, 2240), (2240, 2256), (5328, 5344))


def _permute_w_in_kernel(w_ref, o_ref):
    w = w_ref[...]
    parts = [w[:, a:b] for a, b in EVEN_SRC_COLS]
    used = sum(b - a for a, b in EVEN_SRC_COLS)
    parts.append(jnp.zeros((w.shape[0], U_WIDTH - used), w.dtype))
    o_ref[...] = jnp.concatenate(parts, axis=1).astype(o_ref.dtype)


def _permute_even_w_in(w, rows):
    d, n = w.shape
    return pl.pallas_call(
        _permute_w_in_kernel,
        grid=(d // rows,),
        in_specs=[pl.BlockSpec((rows, n), lambda i: (i, 0))],
        out_specs=pl.BlockSpec((rows, U_WIDTH), lambda i: (i, 0)),
        out_shape=jax.ShapeDtypeStruct((d, U_WIDTH), BF16),
        compiler_params=_params("parallel"),
        name="permute_w_in",
    )(w)


def kernel(x, p, even_norm, even_w_in, even_w_gk, even_b_gk, even_gla_norm, even_w_out,
           odd_norm, odd_w_in, odd_w_out, ffn_norm, ffn_w_gate, ffn_w_up, ffn_w_down,
           ple_norm, ple_w_gate, ple_w_proj, final_norm):
    batch, seq, d = x.shape
    depth = p.shape[0]
    m = batch * seq
    h = x.reshape(m, d)
    tm = _tile(m, 1024)
    tm_row = _tile(m, 512)
    tn_pref = 1792
    blk = _tile(seq, 256)

    def mixer_norm(i):
        return even_norm[i // 2] if i % 2 == 0 else odd_norm[i // 2]

    def in_proj(h, hn, g, w, tn):
        return _norm_matmul(h, g, w, tm, tn) if hn is None else _matmul(hn, w, tm, tn)

    ffn_wg, ffn_wu, ffn_wd = (w.astype(BF16) for w in (ffn_w_gate, ffn_w_up, ffn_w_down))
    ple_wg, ple_wp = ple_w_gate.astype(BF16), ple_w_proj.astype(BF16)

    hn = None
    for i in range(depth):
        j = i // 2
        if i % 2 == 0:
            w_in = _permute_even_w_in(even_w_in[j], _tile(d, 256))
            u = in_proj(h, hn, even_norm[j], w_in, _tile(U_WIDTH, tn_pref))
            o_a = _dsa(u, batch, seq, blk)
            wgk_pad = jnp.zeros((LANES, B_HEADS * B_DK), F32)
            wgk_pad = wgk_pad.at[SM_GK:SM_GK + B_GATE_RANK].set(even_w_gk[j]).astype(BF16)
            o_b = _gla(u, wgk_pad, even_b_gk[j], even_gla_norm[j], batch, seq)
            h, xn = _proj_residual(h, ffn_norm[i], [o_a, o_b], even_w_out[j].astype(BF16), tm_row)
        else:
            qkv = in_proj(h, hn, odd_norm[j], odd_w_in[j].astype(BF16), _tile(3 * d, tn_pref))
            o = _stick_breaking(qkv, batch, seq, blk)
            h, xn = _proj_residual(h, ffn_norm[i], [o], odd_w_out[j].astype(BF16), tm_row)
        y = _ffn(xn, ffn_wg, ffn_wu, ffn_wd, i, tm, _tile(ffn_wg.shape[2], 512))
        last = i == depth - 1
        out = _ple(h, y, ple_norm[i], ple_wg, p.reshape(depth, m, -1), i, ple_wp,
                   final_norm if last else mixer_norm(i + 1), tm_row, last)
        h, hn = (out, None) if last else out
    return h.reshape(batch, seq, d)
```

```python
import functools

import jax
import jax.numpy as jnp
from jax import lax
from jax.experimental import pallas as pl
from jax.experimental.pallas import tpu as pltpu

F32 = jnp.float32
BF16 = jnp.bfloat16
I32 = jnp.int32

EPS = 1e-6
CHUNK = 64
CHUNK_SHIFT = CHUNK.bit_length() - 1
A_HEADS = 8
A_LATENT = 128
A_TOPK_MAX = 256
IDX_HEADS = 16
IDX_DIM = 64
B_HEADS = 4
B_DK = 128
B_DV = 256
B_GATE_RANK = 16
B_GATE_NORM = 16.0
DSA_BLOCK = 512
GLA_CHUNKS_PER_STEP = 4
C_HEADS = 16
LOG2_E = 1.4426950408889634
SB_HEADS_PER_STEP = 4
SB_EXP_UNDERFLOW = -104.0

LANES = 128
SUBLANES = 8
VMEM_LIMIT_BYTES = 56 * 1024 * 1024

U_QA = 0
U_QI = 1024
U_VB = 2048
U_GB = 3072
U_QB = 4096
U_KB = 4608
U_CA = 5120
U_SMALL = 5248
U_WIDTH = 5376
SM_W = 64
SM_GK = 80

NT_DIMS = (((1,), (1,)), ((), ()))
TN_DIMS = (((0,), (0,)), ((), ()))

KEY_NEG_INF = -2139095041
INT_MIN = -2147483648


def _params(*sem):
    return pltpu.CompilerParams(dimension_semantics=sem, vmem_limit_bytes=VMEM_LIMIT_BYTES)


def _tile(n, pref):
    if n <= pref:
        return n
    t = (pref // LANES) * LANES
    while n % t:
        t -= LANES
    return t


def _rms(x, g):
    ms = jnp.mean(x * x, axis=-1, keepdims=True)
    return x * lax.rsqrt(ms + EPS) * g


def _dot(a, b):
    return jnp.dot(a, b, preferred_element_type=F32)


def _dot_nt(a, b):
    return lax.dot_general(a, b, NT_DIMS, preferred_element_type=F32)


def _softplus_tail(z):
    return jnp.log1p(jnp.exp(-jnp.abs(z)))


def _split_bf16(x):
    hi = x.astype(BF16)
    lo = (x - hi.astype(F32)).astype(BF16)
    return hi, lo


def _norm_matmul_kernel(x_ref, g_ref, w_ref, o_ref, xn_ref):
    @pl.when(pl.program_id(1) == 0)
    def _():
        xn_ref[...] = _rms(x_ref[...], g_ref[...]).astype(BF16)

    o_ref[...] = _dot(xn_ref[...], w_ref[...]).astype(o_ref.dtype)


def _norm_matmul(x, g, w, tm, tn):
    m, k = x.shape
    n = w.shape[1]
    return pl.pallas_call(
        _norm_matmul_kernel,
        grid=(m // tm, n // tn),
        in_specs=[
            pl.BlockSpec((tm, k), lambda i, j: (i, 0)),
            pl.BlockSpec((1, k), lambda i, j: (0, 0)),
            pl.BlockSpec((k, tn), lambda i, j: (0, j)),
        ],
        out_specs=pl.BlockSpec((tm, tn), lambda i, j: (i, j)),
        out_shape=jax.ShapeDtypeStruct((m, n), BF16),
        scratch_shapes=[pltpu.VMEM((tm, k), BF16)],
        compiler_params=_params("parallel", "arbitrary"),
        name="norm_matmul",
    )(x, g.reshape(1, k), w)


def _matmul_kernel(x_ref, w_ref, o_ref):
    o_ref[...] = _dot(x_ref[...], w_ref[...]).astype(o_ref.dtype)


def _matmul(xn, w, tm, tn):
    m, k = xn.shape
    n = w.shape[1]
    return pl.pallas_call(
        _matmul_kernel,
        grid=(m // tm, n // tn),
        in_specs=[
            pl.BlockSpec((tm, k), lambda i, j: (i, 0)),
            pl.BlockSpec((k, tn), lambda i, j: (0, j)),
        ],
        out_specs=pl.BlockSpec((tm, tn), lambda i, j: (i, j)),
        out_shape=jax.ShapeDtypeStruct((m, n), BF16),
        compiler_params=_params("parallel", "arbitrary"),
        name="matmul",
    )(xn, w)


def _proj_residual_kernel(*refs, n_in):
    h_ref, g_ref, w_ref = refs[:3]
    a_refs = refs[3:3 + n_in]
    o_ref, xn_ref = refs[3 + n_in:]
    acc = h_ref[...]
    row = 0
    for a_ref in a_refs:
        k = a_ref.shape[1]
        acc = acc + _dot(a_ref[...], w_ref[row:row + k, :])
        row += k
    o_ref[...] = acc
    xn_ref[...] = _rms(acc, g_ref[...]).astype(BF16)


def _resident(shape):
    return pl.BlockSpec(shape, lambda i: (0,) * len(shape), pipeline_mode=pl.Buffered(1))


def _proj_residual(h, g, a_list, w, tm):
    m, n = h.shape
    assert sum(a.shape[1] for a in a_list) == w.shape[0]
    row_block = pl.BlockSpec((tm, n), lambda i: (i, 0))
    in_specs = [row_block, _resident((1, n)), _resident(w.shape)]
    in_specs += [pl.BlockSpec((tm, a.shape[1]), lambda i: (i, 0)) for a in a_list]
    return pl.pallas_call(
        functools.partial(_proj_residual_kernel, n_in=len(a_list)),
        grid=(m // tm,),
        in_specs=in_specs,
        out_specs=[row_block, row_block],
        out_shape=[jax.ShapeDtypeStruct((m, n), F32), jax.ShapeDtypeStruct((m, n), BF16)],
        compiler_params=_params("parallel"),
        name="proj_residual",
    )(h, g.reshape(1, n), w, *a_list)


def _ffn_kernel(xn_ref, wg_ref, wu_ref, wd_ref, o_ref):
    @pl.when(pl.program_id(1) == 0)
    def _():
        o_ref[...] = jnp.zeros(o_ref.shape, F32)

    xn = xn_ref[...]
    gate = _dot(xn, wg_ref[...])
    up = _dot(xn, wu_ref[...])
    act = (gate * jax.nn.sigmoid(gate) * up).astype(BF16)
    o_ref[...] += _dot(act, wd_ref[...])


def _ffn(xn, wg, wu, wd, layer, tm, th):
    m, d = xn.shape
    hid = wg.shape[2]
    return pl.pallas_call(
        _ffn_kernel,
        grid=(m // tm, hid // th),
        in_specs=[
            pl.BlockSpec((tm, d), lambda i, j: (i, 0)),
            pl.BlockSpec((None, d, th), lambda i, j: (layer, 0, j)),
            pl.BlockSpec((None, d, th), lambda i, j: (layer, 0, j)),
            pl.BlockSpec((None, th, d), lambda i, j: (layer, j, 0)),
        ],
        out_specs=pl.BlockSpec((tm, d), lambda i, j: (i, 0)),
        out_shape=jax.ShapeDtypeStruct((m, d), F32),
        compiler_params=_params("parallel", "arbitrary"),
        name="ffn",
    )(xn, wg, wu, wd)


def _ple_kernel(x_ref, y_ref, g_ref, wg_ref, p_ref, wp_ref, ng_ref, *o_refs, last):
    x = x_ref[...] + y_ref[...]
    gate = jax.nn.sigmoid(_dot(_rms(x, g_ref[...]).astype(BF16), wg_ref[...]))
    out = x + _dot(p_ref[...].astype(BF16), wp_ref[...]) * gate
    if last:
        o_refs[0][...] = _rms(out, ng_ref[...])
    else:
        o_refs[0][...] = out
        o_refs[1][...] = _rms(out, ng_ref[...]).astype(BF16)


def _ple(h, y, g, wg, p, layer, wp, next_g, tm, last):
    m, d = h.shape
    pd = p.shape[2]
    row_block = pl.BlockSpec((tm, d), lambda i: (i, 0))
    h_shape = jax.ShapeDtypeStruct((m, d), F32)

    def resident_layer(rows, cols):
        return pl.BlockSpec((None, rows, cols), lambda i: (layer, 0, 0),
                            pipeline_mode=pl.Buffered(1))

    return pl.pallas_call(
        functools.partial(_ple_kernel, last=last),
        grid=(m // tm,),
        in_specs=[
            row_block,
            row_block,
            _resident((1, d)),
            resident_layer(d, d),
            pl.BlockSpec((None, tm, pd), lambda i: (layer, i, 0)),
            resident_layer(pd, d),
            _resident((1, d)),
        ],
        out_specs=row_block if last else [row_block, row_block],
        out_shape=h_shape if last else [h_shape, jax.ShapeDtypeStruct((m, d), BF16)],
        compiler_params=_params("parallel"),
        name="ple",
    )(h, y, g.reshape(1, d), wg, p, wp, next_g.reshape(1, d))


def _key_to_float(key):
    bits = key ^ ((key >> 31) & 0x7FFFFFFF)
    return lax.bitcast_convert_type(bits, F32)


def _dsa_kernel(qi_ref, qsm_ref, ksm_ref, qa_ref, c_ref, o_ref,
                sc_ref, ct_ref, acc_ref, m_ref, l_ref, *, topk, qb_size, key_tile, seq):
    tq, tk = qb_size, key_tile
    qb = pl.program_id(1)
    nk = (qb + 1) * (tq // tk)

    @pl.when(qb == 0)
    def _():
        ct_ref[...] = jnp.transpose(c_ref[...].astype(F32)).astype(BF16)

    row = lax.broadcasted_iota(I32, (tk, tq), 0)
    col = lax.broadcasted_iota(I32, (tk, tq), 1)
    limit = (((qb * tq + col) >> CHUNK_SHIFT) + 1) << CHUNK_SHIFT

    w_t = jnp.transpose(qsm_ref[...].astype(F32))[SM_W:SM_W + IDX_HEADS, :]

    def score_tile(kt, carry):
        rows = pl.ds(pl.multiple_of(kt * tk, tk), tk)
        kk = ksm_ref[rows, :][:, :IDX_DIM]
        sc = jnp.zeros((tk, tq), F32)
        for h in range(IDX_HEADS):
            qh = qi_ref[:, h * IDX_DIM:(h + 1) * IDX_DIM]
            sc = sc + jnp.maximum(_dot_nt(kk, qh), 0.0) * w_t[h:h + 1, :]
        sc = sc * (IDX_DIM ** -0.5 * IDX_HEADS ** -0.5)
        sc_ref[rows, :] = jnp.where(kt * tk + row < limit, sc, -jnp.inf)
        return carry

    lax.fori_loop(0, nk, score_tile, 0)

    def count(pred):
        def body(kt, acc):
            rows = pl.ds(pl.multiple_of(kt * tk, tk), tk)
            hit = pred(sc_ref[rows, :]).astype(I32)
            return acc + jnp.sum(hit.reshape(tk // SUBLANES, SUBLANES, tq), axis=0)
        acc = lax.fori_loop(0, nk, body, jnp.zeros((SUBLANES, tq), I32))
        return jnp.sum(acc, axis=0, keepdims=True)

    def count_ge(key):
        cand = _key_to_float(key)
        return count(lambda s: s >= cand)

    zero_key = jnp.zeros((1, tq), I32)
    res0 = jnp.where(count_ge(zero_key) >= topk, 0, INT_MIN).astype(I32)

    def bisect(i, res):
        cand = res | jnp.left_shift(jnp.int32(1), 30 - i)
        return jnp.where(count_ge(cand) >= topk, cand, res)

    res = lax.fori_loop(0, 31, bisect, res0)
    thr = _key_to_float(jnp.maximum(res, KEY_NEG_INF))
    n_ge = count(lambda s: s >= thr) + jnp.where(thr == -jnp.inf, seq - nk * tk, 0)

    @pl.when(jnp.max(n_ge) > topk)
    def _():
        need = (topk - count(lambda s: s > thr)).astype(F32)
        tri = (lax.broadcasted_iota(I32, (tk, tk), 0)
               >= lax.broadcasted_iota(I32, (tk, tk), 1)).astype(BF16)

        def body(kt, seen):
            rows = pl.ds(pl.multiple_of(kt * tk, tk), tk)
            s = sc_ref[rows, :]
            eq = s == thr
            rank = _dot(tri, eq.astype(BF16)) + seen
            drop = (eq & (rank > need)) | (kt * tk + row >= limit)
            sc_ref[rows, :] = jnp.where(drop, jnp.nan, s)
            return seen + jnp.sum(eq.astype(F32), axis=0, keepdims=True)

        lax.fori_loop(0, nk, body, jnp.zeros((1, tq), F32))

    m_ref[...] = jnp.full(m_ref.shape, -1e30, F32)
    l_ref[...] = jnp.zeros(l_ref.shape, F32)
    acc_ref[...] = jnp.zeros(acc_ref.shape, F32)

    heads = range(A_HEADS)
    hsl = [slice(h * A_LATENT, (h + 1) * A_LATENT) for h in heads]
    c2 = (A_LATENT ** -0.5) * LOG2_E

    def attend(kt, carry):
        rows = pl.ds(pl.multiple_of(kt * tk, tk), tk)
        valid = sc_ref[rows, :] >= thr
        ck = c_ref[rows, :]
        ckt = ct_ref[:, rows]
        raws = [jnp.where(valid, _dot_nt(ck, qa_ref[:, hs]), -1e30) for hs in hsl]
        ps, alphas = [], []
        for h, raw in zip(heads, raws):
            m_old = m_ref[h:h + 1, :]
            m_new = jnp.maximum(m_old, jnp.max(raw, axis=0, keepdims=True))
            alpha = jnp.exp2((m_old - m_new) * c2)
            p = jnp.exp2((raw - m_new) * c2)
            l_ref[h:h + 1, :] = alpha * l_ref[h:h + 1, :] + jnp.sum(p, axis=0, keepdims=True)
            m_ref[h:h + 1, :] = m_new
            ps.append(p.astype(BF16))
            alphas.append(alpha)
        pvs = [_dot(ckt, p) for p in ps]
        for hs, alpha, pv in zip(hsl, alphas, pvs):
            acc_ref[hs, :] = alpha * acc_ref[hs, :] + pv
        return carry

    lax.fori_loop(0, nk, attend, 0)

    for h in heads:
        out = acc_ref[hsl[h], :] / l_ref[h:h + 1, :]
        o_ref[:, hsl[h]] = jnp.transpose(out).astype(o_ref.dtype)


def _dsa(u, batch, seq, qb_size, key_tile):
    m = batch * seq
    nqb = seq // qb_size
    assert qb_size % key_tile == 0 and key_tile % CHUNK == 0
    topk = min(A_TOPK_MAX, seq // 4)
    wide = A_HEADS * A_LATENT
    return pl.pallas_call(
        functools.partial(_dsa_kernel, topk=topk, qb_size=qb_size, key_tile=key_tile, seq=seq),
        grid=(batch, nqb),
        in_specs=[
            pl.BlockSpec((qb_size, wide), lambda b, q: (b * nqb + q, U_QI // wide)),
            pl.BlockSpec((qb_size, LANES), lambda b, q: (b * nqb + q, U_SMALL // LANES)),
            pl.BlockSpec((seq, LANES), lambda b, q: (b, U_SMALL // LANES)),
            pl.BlockSpec((qb_size, wide), lambda b, q: (b * nqb + q, U_QA // wide)),
            pl.BlockSpec((seq, LANES), lambda b, q: (b, U_CA // LANES)),
        ],
        out_specs=pl.BlockSpec((qb_size, wide), lambda b, q: (b * nqb + q, 0)),
        out_shape=jax.ShapeDtypeStruct((m, wide), BF16),
        scratch_shapes=[
            pltpu.VMEM((seq, qb_size), F32),
            pltpu.VMEM((A_LATENT, seq), BF16),
            pltpu.VMEM((wide, qb_size), F32),
            pltpu.VMEM((A_HEADS, qb_size), F32),
            pltpu.VMEM((A_HEADS, qb_size), F32),
        ],
        compiler_params=_params("parallel", "arbitrary"),
        name="dsa",
    )(u, u, u, u, u)


def _gla_kernel(q_ref, k_ref, v_ref, g_ref, sm_ref, wgk_ref, bgk_ref, gn_ref, o_ref, st_ref,
                *, nchunks):
    c = CHUNK
    n = GLA_CHUNKS_PER_STEP
    span = n * c
    row = lax.broadcasted_iota(I32, (span, span), 0)
    col = lax.broadcasted_iota(I32, (span, span), 1)
    tri = ((row >= col) & ((row >> CHUNK_SHIFT) == (col >> CHUNK_SHIFT))).astype(BF16)
    causal = lax.broadcasted_iota(I32, (c, c), 0) >= lax.broadcasted_iota(I32, (c, c), 1)
    st_ref[...] = jnp.zeros(st_ref.shape, F32)
    wgk = wgk_ref[...]
    bgk = bgk_ref[...]
    gn = gn_ref[...]
    heads = range(B_HEADS)
    ksl = [slice(h * B_DK, (h + 1) * B_DK) for h in heads]
    vsl = [slice(h * B_DV, (h + 1) * B_DV) for h in heads]
    work = [(s, h) for s in range(n) for h in heads]

    def step(si, carry):
        base = pl.multiple_of(si * span, span)
        rows = pl.ds(base, span)
        x = _dot(sm_ref[rows, :], wgk) + bgk
        gk = -(jnp.maximum(-x, 0.0) + _softplus_tail(x)) / B_GATE_NORM
        hi, lo = _split_bf16(gk)
        cum_all = _dot(tri, hi) + _dot(tri, lo)
        q_es, k_es, k_ds, decays, vs_ = {}, {}, {}, {}, {}
        for s, h in work:
            sub = pl.ds(base + s * c, c)
            cum = cum_all[s * c:(s + 1) * c, ksl[h]]
            last = cum[c - 1:c, :]
            q = q_ref[sub, ksl[h]].astype(F32) * (B_DK ** -0.5)
            k = k_ref[sub, ksl[h]].astype(F32)
            q_es[s, h] = (q * jnp.exp(cum)).astype(BF16)
            k_es[s, h] = (k * jnp.exp(-cum)).astype(BF16)
            k_ds[s, h] = (k * jnp.exp(last - cum)).astype(BF16)
            decays[s, h] = jnp.exp(last)
            vs_[s, h] = v_ref[sub, vsl[h]]
        atts = {w: _dot_nt(q_es[w], k_es[w]) for w in work}
        kvs = {w: lax.dot_general(vs_[w], k_ds[w], TN_DIMS, preferred_element_type=F32)
               for w in work}
        sts = {h: st_ref[h] for h in heads}
        inter = {}
        for s in range(n):
            for h in heads:
                inter[s, h] = _dot_nt(q_es[s, h], sts[h].astype(BF16))
            for h in heads:
                sts[h] = sts[h] * decays[s, h] + kvs[s, h]
        for h in heads:
            st_ref[h] = sts[h]
        atts = {w: jnp.where(causal, atts[w], 0.0).astype(BF16) for w in work}
        intra = {w: _dot(atts[w], vs_[w]) for w in work}
        for s, h in work:
            sub = pl.ds(base + s * c, c)
            gate = g_ref[sub, vsl[h]].astype(F32)
            o = intra[s, h] + inter[s, h]
            o_ref[sub, vsl[h]] = (_rms(o, gn) * (gate * jax.nn.sigmoid(gate))).astype(o_ref.dtype)
        return carry

    lax.fori_loop(0, nchunks // n, step, 0)


def _gla(u, wgk_pad, bgk, gnorm, batch, seq):
    m = batch * seq
    qk_wide = B_HEADS * B_DK
    v_wide = B_HEADS * B_DV
    return pl.pallas_call(
        functools.partial(_gla_kernel, nchunks=seq // CHUNK),
        grid=(batch,),
        in_specs=[
            pl.BlockSpec((seq, qk_wide), lambda b: (b, U_QB // qk_wide)),
            pl.BlockSpec((seq, qk_wide), lambda b: (b, U_KB // qk_wide)),
            pl.BlockSpec((seq, v_wide), lambda b: (b, U_VB // v_wide)),
            pl.BlockSpec((seq, v_wide), lambda b: (b, U_GB // v_wide)),
            pl.BlockSpec((seq, LANES), lambda b: (b, U_SMALL // LANES)),
            pl.BlockSpec((LANES, qk_wide), lambda b: (0, 0)),
            pl.BlockSpec((1, qk_wide), lambda b: (0, 0)),
            pl.BlockSpec((1, B_DV), lambda b: (0, 0)),
        ],
        out_specs=pl.BlockSpec((seq, v_wide), lambda b: (b, 0)),
        out_shape=jax.ShapeDtypeStruct((m, v_wide), BF16),
        scratch_shapes=[pltpu.VMEM((B_HEADS, B_DV, B_DK), F32)],
        compiler_params=_params("parallel"),
        name="gla",
    )(u, u, u, u, u, wgk_pad, bgk.reshape(1, -1), gnorm.reshape(1, -1))


def _sb_kernel(q_ref, k_ref, v_ref, o_ref, *, blk, nblk, head_dim, heads):
    t = blk
    d = head_dim
    row = lax.broadcasted_iota(I32, (t, t), 0)
    col = lax.broadcasted_iota(I32, (t, t), 1)
    upper = (row > col).astype(BF16)
    diag = col < row
    scale = d ** -0.5

    hsl = [slice(h * d, (h + 1) * d) for h in range(heads)]

    def block(qrows, krows, runs, mask):
        zs = [_dot_nt(q_ref[qrows, hs], k_ref[krows, hs]) * scale for hs in hsl]
        lms, log_bs = [], []
        for z in zs:
            log_1mb = -(jnp.maximum(z, 0.0) + jnp.log(1.0 + jnp.exp(-jnp.abs(z))))
            log_bs.append(log_1mb + z)
            lms.append(log_1mb if mask is None else jnp.where(mask, log_1mb, 0.0))
        sufs = [_dot(lm.astype(BF16), upper) for lm in lms]
        ws = []
        for suf, log_b, run in zip(sufs, log_bs, runs):
            w = jnp.exp(suf + run + log_b)
            ws.append((w if mask is None else jnp.where(mask, w, 0.0)).astype(BF16))
        outs = [_dot(w, v_ref[krows, hs]) for w, hs in zip(ws, hsl)]
        new_runs = [run + jnp.sum(lm, axis=1, keepdims=True) for run, lm in zip(runs, lms)]
        return new_runs, outs

    def q_block(qb, carry):
        qrows = pl.ds(pl.multiple_of(qb * t, t), t)
        runs, accs = block(qrows, qrows, [jnp.zeros((t, 1), F32)] * heads, diag)

        def alive(rs):
            top = jnp.max(rs[0])
            for r in rs[1:]:
                top = jnp.maximum(top, jnp.max(r))
            return (top >= SB_EXP_UNDERFLOW).astype(I32)

        def cond(state):
            i, live, _, _ = state
            return jnp.logical_and(i <= qb, live > 0)

        def k_block(state):
            i, _, rs, acs = state
            krows = pl.ds(pl.multiple_of((qb - i) * t, t), t)
            new_rs, outs = block(qrows, krows, list(rs), None)
            new_acs = [a + o for a, o in zip(acs, outs)]
            return i + 1, alive(new_rs), tuple(new_rs), tuple(new_acs)

        state = (jnp.int32(1), alive(runs), tuple(runs), tuple(accs))
        _, _, _, accs = lax.while_loop(cond, k_block, state)
        for h in range(heads):
            o_ref[qrows, h * d:(h + 1) * d] = accs[h].astype(o_ref.dtype)
        return carry

    lax.fori_loop(0, nblk, q_block, 0)


def _stick_breaking(qkv, batch, seq, blk):
    m = batch * seq
    d = qkv.shape[1] // (3 * C_HEADS)
    groups = C_HEADS // SB_HEADS_PER_STEP
    wide = SB_HEADS_PER_STEP * d
    return pl.pallas_call(
        functools.partial(_sb_kernel, blk=blk, nblk=seq // blk, head_dim=d, heads=SB_HEADS_PER_STEP),
        grid=(batch, groups),
        in_specs=[
            pl.BlockSpec((seq, wide), lambda b, g: (b, g)),
            pl.BlockSpec((seq, wide), lambda b, g: (b, groups + g)),
            pl.BlockSpec((seq, wide), lambda b, g: (b, 2 * groups + g)),
        ],
        out_specs=pl.BlockSpec((seq, wide), lambda b, g: (b, g)),
        out_shape=jax.ShapeDtypeStruct((m, C_HEADS * d), BF16),
        compiler_params=_params("parallel", "arbitrary"),
        name="stick_breaking",
    )(qkv, qkv, qkv)


EVEN_SRC_COLS = ((0, 1024), (1152, 2176), (3280, 4304), (4304, 5328), (2256, 2768),
                 (2768, 3280), (1024, 1152), (2176, 2240), (2240, 2256), (5328, 5344))


def _permute_w_in_kernel(w_ref, o_ref):
    w = w_ref[...]
    parts = [w[:, a:b] for a, b in EVEN_SRC_COLS]
    used = sum(b - a for a, b in EVEN_SRC_COLS)
    parts.append(jnp.zeros((w.shape[0], U_WIDTH - used), w.dtype))
    o_ref[...] = jnp.concatenate(parts, axis=1).astype(o_ref.dtype)


def _permute_even_w_in(w, rows):
    d, n = w.shape
    return pl.pallas_call(
        _permute_w_in_kernel,
        grid=(d // rows,),
        in_specs=[pl.BlockSpec((rows, n), lambda i: (i, 0))],
        out_specs=pl.BlockSpec((rows, U_WIDTH), lambda i: (i, 0)),
        out_shape=jax.ShapeDtypeStruct((d, U_WIDTH), BF16),
        compiler_params=_params("parallel"),
        name="permute_w_in",
    )(w)


def kernel(x, p, even_norm, even_w_in, even_w_gk, even_b_gk, even_gla_norm, even_w_out,
           odd_norm, odd_w_in, odd_w_out, ffn_norm, ffn_w_gate, ffn_w_up, ffn_w_down,
           ple_norm, ple_w_gate, ple_w_proj, final_norm):
    batch, seq, d = x.shape
    depth = p.shape[0]
    m = batch * seq
    h = x.reshape(m, d)
    tm = _tile(m, 1024)
    tm_row = _tile(m, 512)
    tn_pref = 1792
    blk = _tile(seq, 256)
    dsa_blk = _tile(seq, DSA_BLOCK)

    def mixer_norm(i):
        return even_norm[i // 2] if i % 2 == 0 else odd_norm[i // 2]

    def in_proj(h, hn, g, w, tn):
        return _norm_matmul(h, g, w, tm, tn) if hn is None else _matmul(hn, w, tm, tn)

    ffn_wg, ffn_wu, ffn_wd = (w.astype(BF16) for w in (ffn_w_gate, ffn_w_up, ffn_w_down))
    ple_wg, ple_wp = ple_w_gate.astype(BF16), ple_w_proj.astype(BF16)

    hn = None
    for i in range(depth):
        j = i // 2
        if i % 2 == 0:
            w_in = _permute_even_w_in(even_w_in[j], _tile(d, 256))
            u = in_proj(h, hn, even_norm[j], w_in, _tile(U_WIDTH, tn_pref))
            o_a = _dsa(u, batch, seq, dsa_blk, dsa_blk)
            wgk_pad = jnp.zeros((LANES, B_HEADS * B_DK), F32)
            wgk_pad = wgk_pad.at[SM_GK:SM_GK + B_GATE_RANK].set(even_w_gk[j]).astype(BF16)
            o_b = _gla(u, wgk_pad, even_b_gk[j], even_gla_norm[j], batch, seq)
            h, xn = _proj_residual(h, ffn_norm[i], [o_a, o_b], even_w_out[j].astype(BF16), tm_row)
        else:
            qkv = in_proj(h, hn, odd_norm[j], odd_w_in[j].astype(BF16), _tile(3 * d, tn_pref))
            o = _stick_breaking(qkv, batch, seq, blk)
            h, xn = _proj_residual(h, ffn_norm[i], [o], odd_w_out[j].astype(BF16), tm_row)
        y = _ffn(xn, ffn_wg, ffn_wu, ffn_wd, i, tm, _tile(ffn_wg.shape[2], 512))
        last = i == depth - 1
        out = _ple(h, y, ple_norm[i], ple_wg, p.reshape(depth, m, -1), i, ple_wp,
                   final_norm if last else mixer_norm(i + 1), tm_row, last)
        h, hn = (out, None) if last else out
    return h.reshape(batch, seq, d)
```

```python
import functools

import jax
import jax.numpy as jnp
from jax import lax
from jax.experimental import pallas as pl
from jax.experimental.pallas import tpu as pltpu

F32 = jnp.float32
BF16 = jnp.bfloat16
I32 = jnp.int32

EPS = 1e-6
CHUNK = 64
CHUNK_SHIFT = CHUNK.bit_length() - 1
A_HEADS = 8
A_LATENT = 128
A_TOPK_MAX = 256
IDX_HEADS = 16
IDX_DIM = 64
B_HEADS = 4
B_DK = 128
B_DV = 256
B_GATE_RANK = 16
B_GATE_NORM = 16.0
DSA_BLOCK = 512
GLA_CHUNKS_PER_STEP = 4
C_HEADS = 16
LOG2_E = 1.4426950408889634
SB_HEADS_PER_STEP = 4
SB_EXP_UNDERFLOW = -104.0

LANES = 128
SUBLANES = 8
VMEM_LIMIT_BYTES = 56 * 1024 * 1024

U_QA = 0
U_QI = 1024
U_VB = 2048
U_GB = 3072
U_QB = 4096
U_KB = 4608
U_CA = 5120
U_SMALL = 5248
U_WIDTH = 5376
SM_W = 64
SM_GK = 80

NT_DIMS = (((1,), (1,)), ((), ()))
TN_DIMS = (((0,), (0,)), ((), ()))

KEY_NEG_INF = -2139095041
INT_MIN = -2147483648


def _params(*sem):
    return pltpu.CompilerParams(dimension_semantics=sem, vmem_limit_bytes=VMEM_LIMIT_BYTES)


def _tile(n, pref):
    if n <= pref:
        return n
    t = (pref // LANES) * LANES
    while n % t:
        t -= LANES
    return t


def _rms(x, g):
    ms = jnp.mean(x * x, axis=-1, keepdims=True)
    return x * lax.rsqrt(ms + EPS) * g


def _dot(a, b):
    return jnp.dot(a, b, preferred_element_type=F32)


def _dot_nt(a, b):
    return lax.dot_general(a, b, NT_DIMS, preferred_element_type=F32)


def _softplus_tail(z):
    return jnp.log1p(jnp.exp(-jnp.abs(z)))


def _split_bf16(x):
    hi = x.astype(BF16)
    lo = (x - hi.astype(F32)).astype(BF16)
    return hi, lo


def _norm_matmul_kernel(x_ref, g_ref, w_ref, o_ref, xn_ref):
    @pl.when(pl.program_id(1) == 0)
    def _():
        xn_ref[...] = _rms(x_ref[...], g_ref[...]).astype(BF16)

    o_ref[...] = _dot(xn_ref[...], w_ref[...]).astype(o_ref.dtype)


def _norm_matmul(x, g, w, tm, tn):
    m, k = x.shape
    n = w.shape[1]
    return pl.pallas_call(
        _norm_matmul_kernel,
        grid=(m // tm, n // tn),
        in_specs=[
            pl.BlockSpec((tm, k), lambda i, j: (i, 0)),
            pl.BlockSpec((1, k), lambda i, j: (0, 0)),
            pl.BlockSpec((k, tn), lambda i, j: (0, j)),
        ],
        out_specs=pl.BlockSpec((tm, tn), lambda i, j: (i, j)),
        out_shape=jax.ShapeDtypeStruct((m, n), BF16),
        scratch_shapes=[pltpu.VMEM((tm, k), BF16)],
        compiler_params=_params("parallel", "arbitrary"),
        name="norm_matmul",
    )(x, g.reshape(1, k), w)


def _matmul_kernel(x_ref, w_ref, o_ref):
    o_ref[...] = _dot(x_ref[...], w_ref[...]).astype(o_ref.dtype)


def _matmul(xn, w, tm, tn):
    m, k = xn.shape
    n = w.shape[1]
    return pl.pallas_call(
        _matmul_kernel,
        grid=(m // tm, n // tn),
        in_specs=[
            pl.BlockSpec((tm, k), lambda i, j: (i, 0)),
            pl.BlockSpec((k, tn), lambda i, j: (0, j)),
        ],
        out_specs=pl.BlockSpec((tm, tn), lambda i, j: (i, j)),
        out_shape=jax.ShapeDtypeStruct((m, n), BF16),
        compiler_params=_params("parallel", "arbitrary"),
        name="matmul",
    )(xn, w)


def _proj_residual_kernel(*refs, n_in):
    h_ref, g_ref, w_ref = refs[:3]
    a_refs = refs[3:3 + n_in]
    o_ref, xn_ref = refs[3 + n_in:]
    acc = h_ref[...]
    row = 0
    for a_ref in a_refs:
        k = a_ref.shape[1]
        acc = acc + _dot(a_ref[...], w_ref[row:row + k, :])
        row += k
    o_ref[...] = acc
    xn_ref[...] = _rms(acc, g_ref[...]).astype(BF16)


def _resident(shape):
    return pl.BlockSpec(shape, lambda i: (0,) * len(shape), pipeline_mode=pl.Buffered(1))


def _proj_residual(h, g, a_list, w, tm):
    m, n = h.shape
    assert sum(a.shape[1] for a in a_list) == w.shape[0]
    row_block = pl.BlockSpec((tm, n), lambda i: (i, 0))
    in_specs = [row_block, _resident((1, n)), _resident(w.shape)]
    in_specs += [pl.BlockSpec((tm, a.shape[1]), lambda i: (i, 0)) for a in a_list]
    return pl.pallas_call(
        functools.partial(_proj_residual_kernel, n_in=len(a_list)),
        grid=(m // tm,),
        in_specs=in_specs,
        out_specs=[row_block, row_block],
        out_shape=[jax.ShapeDtypeStruct((m, n), F32), jax.ShapeDtypeStruct((m, n), BF16)],
        compiler_params=_params("parallel"),
        name="proj_residual",
    )(h, g.reshape(1, n), w, *a_list)


def _ffn_kernel(xn_ref, wg_ref, wu_ref, wd_ref, o_ref):
    @pl.when(pl.program_id(1) == 0)
    def _():
        o_ref[...] = jnp.zeros(o_ref.shape, F32)

    xn = xn_ref[...]
    gate = _dot(xn, wg_ref[...])
    up = _dot(xn, wu_ref[...])
    act = (gate * jax.nn.sigmoid(gate) * up).astype(BF16)
    o_ref[...] += _dot(act, wd_ref[...])


def _ffn(xn, wg, wu, wd, layer, tm, th):
    m, d = xn.shape
    hid = wg.shape[2]
    return pl.pallas_call(
        _ffn_kernel,
        grid=(m // tm, hid // th),
        in_specs=[
            pl.BlockSpec((tm, d), lambda i, j: (i, 0)),
            pl.BlockSpec((None, d, th), lambda i, j: (layer, 0, j)),
            pl.BlockSpec((None, d, th), lambda i, j: (layer, 0, j)),
            pl.BlockSpec((None, th, d), lambda i, j: (layer, j, 0)),
        ],
        out_specs=pl.BlockSpec((tm, d), lambda i, j: (i, 0)),
        out_shape=jax.ShapeDtypeStruct((m, d), F32),
        compiler_params=_params("parallel", "arbitrary"),
        name="ffn",
    )(xn, wg, wu, wd)


def _ple_kernel(x_ref, y_ref, g_ref, wg_ref, p_ref, wp_ref, ng_ref, *o_refs, last):
    x = x_ref[...] + y_ref[...]
    gate = jax.nn.sigmoid(_dot(_rms(x, g_ref[...]).astype(BF16), wg_ref[...]))
    out = x + _dot(p_ref[...].astype(BF16), wp_ref[...]) * gate
    if last:
        o_refs[0][...] = _rms(out, ng_ref[...])
    else:
        o_refs[0][...] = out
        o_refs[1][...] = _rms(out, ng_ref[...]).astype(BF16)


def _ple(h, y, g, wg, p, layer, wp, next_g, tm, last):
    m, d = h.shape
    pd = p.shape[2]
    row_block = pl.BlockSpec((tm, d), lambda i: (i, 0))
    h_shape = jax.ShapeDtypeStruct((m, d), F32)

    def resident_layer(rows, cols):
        return pl.BlockSpec((None, rows, cols), lambda i: (layer, 0, 0),
                            pipeline_mode=pl.Buffered(1))

    return pl.pallas_call(
        functools.partial(_ple_kernel, last=last),
        grid=(m // tm,),
        in_specs=[
            row_block,
            row_block,
            _resident((1, d)),
            resident_layer(d, d),
            pl.BlockSpec((None, tm, pd), lambda i: (layer, i, 0)),
            resident_layer(pd, d),
            _resident((1, d)),
        ],
        out_specs=row_block if last else [row_block, row_block],
        out_shape=h_shape if last else [h_shape, jax.ShapeDtypeStruct((m, d), BF16)],
        compiler_params=_params("parallel"),
        name="ple",
    )(h, y, g.reshape(1, d), wg, p, wp, next_g.reshape(1, d))


def _key_to_float(key):
    bits = key ^ ((key >> 31) & 0x7FFFFFFF)
    return lax.bitcast_convert_type(bits, F32)


def _dsa_kernel(qi_ref, qsm_ref, ksm_ref, qa_ref, c_ref, o_ref,
                sc_ref, ct_ref, acc_ref, m_ref, l_ref, *, topk, qb_size, key_tile, seq):
    tq, tk = qb_size, key_tile
    qb = pl.program_id(1)
    nk = (qb + 1) * (tq // tk)

    @pl.when(qb == 0)
    def _():
        ct_ref[...] = jnp.transpose(c_ref[...].astype(F32)).astype(BF16)

    row = lax.broadcasted_iota(I32, (tk, tq), 0)
    col = lax.broadcasted_iota(I32, (tk, tq), 1)
    limit = (((qb * tq + col) >> CHUNK_SHIFT) + 1) << CHUNK_SHIFT

    w_t = jnp.transpose(qsm_ref[...].astype(F32))[SM_W:SM_W + IDX_HEADS, :]

    def score_tile(kt, carry):
        rows = pl.ds(pl.multiple_of(kt * tk, tk), tk)
        kk = ksm_ref[rows, :][:, :IDX_DIM]
        sc = jnp.zeros((tk, tq), F32)
        for h in range(IDX_HEADS):
            qh = qi_ref[:, h * IDX_DIM:(h + 1) * IDX_DIM]
            sc = sc + jnp.maximum(_dot_nt(kk, qh), 0.0) * w_t[h:h + 1, :]
        sc = sc * (IDX_DIM ** -0.5 * IDX_HEADS ** -0.5)
        sc_ref[rows, :] = jnp.where(kt * tk + row < limit, sc, -jnp.inf)
        return carry

    lax.fori_loop(0, nk, score_tile, 0)

    def count(pred):
        def body(kt, acc):
            rows = pl.ds(pl.multiple_of(kt * tk, tk), tk)
            hit = pred(sc_ref[rows, :]).astype(I32)
            return acc + jnp.sum(hit.reshape(tk // SUBLANES, SUBLANES, tq), axis=0)
        acc = lax.fori_loop(0, nk, body, jnp.zeros((SUBLANES, tq), I32))
        return jnp.sum(acc, axis=0, keepdims=True)

    def count_ge(key):
        cand = _key_to_float(key)
        return count(lambda s: s >= cand)

    zero_key = jnp.zeros((1, tq), I32)
    res0 = jnp.where(count_ge(zero_key) >= topk, 0, INT_MIN).astype(I32)

    def bisect(i, res):
        cand = res | jnp.left_shift(jnp.int32(1), 30 - i)
        return jnp.where(count_ge(cand) >= topk, cand, res)

    res = lax.fori_loop(0, 31, bisect, res0)
    thr = _key_to_float(jnp.maximum(res, KEY_NEG_INF))
    n_ge = count(lambda s: s >= thr) + jnp.where(thr == -jnp.inf, seq - nk * tk, 0)

    @pl.when(jnp.max(n_ge) > topk)
    def _():
        need = (topk - count(lambda s: s > thr)).astype(F32)
        tri = (lax.broadcasted_iota(I32, (tk, tk), 0)
               >= lax.broadcasted_iota(I32, (tk, tk), 1)).astype(BF16)

        def body(kt, seen):
            rows = pl.ds(pl.multiple_of(kt * tk, tk), tk)
            s = sc_ref[rows, :]
            eq = s == thr
            rank = _dot(tri, eq.astype(BF16)) + seen
            drop = (eq & (rank > need)) | (kt * tk + row >= limit)
            sc_ref[rows, :] = jnp.where(drop, jnp.nan, s)
            return seen + jnp.sum(eq.astype(F32), axis=0, keepdims=True)

        lax.fori_loop(0, nk, body, jnp.zeros((1, tq), F32))

    m_ref[...] = jnp.full(m_ref.shape, -1e30, F32)
    l_ref[...] = jnp.zeros(l_ref.shape, F32)
    acc_ref[...] = jnp.zeros(acc_ref.shape, F32)

    heads = range(A_HEADS)
    hsl = [slice(h * A_LATENT, (h + 1) * A_LATENT) for h in heads]
    c2 = (A_LATENT ** -0.5) * LOG2_E

    def attend(kt, carry):
        rows = pl.ds(pl.multiple_of(kt * tk, tk), tk)
        valid = sc_ref[rows, :] >= thr
        ck = c_ref[rows, :]
        ckt = ct_ref[:, rows]
        raws = [jnp.where(valid, _dot_nt(ck, qa_ref[:, hs]), -1e30) for hs in hsl]
        ps, alphas = [], []
        for h, raw in zip(heads, raws):
            m_old = m_ref[h:h + 1, :]
            m_new = jnp.maximum(m_old, jnp.max(raw, axis=0, keepdims=True))
            alpha = jnp.exp2((m_old - m_new) * c2)
            p = jnp.exp2((raw - m_new) * c2)
            l_ref[h:h + 1, :] = alpha * l_ref[h:h + 1, :] + jnp.sum(p, axis=0, keepdims=True)
            m_ref[h:h + 1, :] = m_new
            ps.append(p.astype(BF16))
            alphas.append(alpha)
        pvs = [_dot(ckt, p) for p in ps]
        for hs, alpha, pv in zip(hsl, alphas, pvs):
            acc_ref[hs, :] = alpha * acc_ref[hs, :] + pv
        return carry

    lax.fori_loop(0, nk, attend, 0)

    for h in heads:
        out = acc_ref[hsl[h], :] / l_ref[h:h + 1, :]
        o_ref[:, hsl[h]] = jnp.transpose(out).astype(o_ref.dtype)


def _dsa(u, batch, seq, qb_size, key_tile):
    m = batch * seq
    nqb = seq // qb_size
    assert qb_size % key_tile == 0 and key_tile % CHUNK == 0
    topk = min(A_TOPK_MAX, seq // 4)
    wide = A_HEADS * A_LATENT
    return pl.pallas_call(
        functools.partial(_dsa_kernel, topk=topk, qb_size=qb_size, key_tile=key_tile, seq=seq),
        grid=(batch, nqb),
        in_specs=[
            pl.BlockSpec((qb_size, wide), lambda b, q: (b * nqb + q, U_QI // wide)),
            pl.BlockSpec((qb_size, LANES), lambda b, q: (b * nqb + q, U_SMALL // LANES)),
            pl.BlockSpec((seq, LANES), lambda b, q: (b, U_SMALL // LANES)),
            pl.BlockSpec((qb_size, wide), lambda b, q: (b * nqb + q, U_QA // wide)),
            pl.BlockSpec((seq, LANES), lambda b, q: (b, U_CA // LANES)),
        ],
        out_specs=pl.BlockSpec((qb_size, wide), lambda b, q: (b * nqb + q, 0)),
        out_shape=jax.ShapeDtypeStruct((m, wide), BF16),
        scratch_shapes=[
            pltpu.VMEM((seq, qb_size), F32),
            pltpu.VMEM((A_LATENT, seq), BF16),
            pltpu.VMEM((wide, qb_size), F32),
            pltpu.VMEM((A_HEADS, qb_size), F32),
            pltpu.VMEM((A_HEADS, qb_size), F32),
        ],
        compiler_params=_params("parallel", "arbitrary"),
        name="dsa",
    )(u, u, u, u, u)


def _gla_kernel(q_ref, k_ref, v_ref, g_ref, sm_ref, wgk_ref, bgk_ref, gn_ref, o_ref, st_ref,
                *, nchunks):
    c = CHUNK
    n = GLA_CHUNKS_PER_STEP
    span = n * c
    row = lax.broadcasted_iota(I32, (span, span), 0)
    col = lax.broadcasted_iota(I32, (span, span), 1)
    tri = ((row >= col) & ((row >> CHUNK_SHIFT) == (col >> CHUNK_SHIFT))).astype(BF16)
    causal = lax.broadcasted_iota(I32, (c, c), 0) >= lax.broadcasted_iota(I32, (c, c), 1)
    st_ref[...] = jnp.zeros(st_ref.shape, F32)
    wgk = wgk_ref[...]
    bgk = bgk_ref[...]
    gn = gn_ref[...]
    heads = range(B_HEADS)
    ksl = [slice(h * B_DK, (h + 1) * B_DK) for h in heads]
    vsl = [slice(h * B_DV, (h + 1) * B_DV) for h in heads]
    work = [(s, h) for s in range(n) for h in heads]

    def step(si, carry):
        base = pl.multiple_of(si * span, span)
        rows = pl.ds(base, span)
        x = _dot(sm_ref[rows, :], wgk) + bgk
        gk = -(jnp.maximum(-x, 0.0) + _softplus_tail(x)) / B_GATE_NORM
        hi, lo = _split_bf16(gk)
        cum_all = _dot(tri, hi) + _dot(tri, lo)
        q_es, k_es, k_ds, decays, vs_ = {}, {}, {}, {}, {}
        for s, h in work:
            sub = pl.ds(base + s * c, c)
            cum = cum_all[s * c:(s + 1) * c, ksl[h]]
            last = cum[c - 1:c, :]
            q = q_ref[sub, ksl[h]].astype(F32) * (B_DK ** -0.5)
            k = k_ref[sub, ksl[h]].astype(F32)
            q_es[s, h] = (q * jnp.exp(cum)).astype(BF16)
            k_es[s, h] = (k * jnp.exp(-cum)).astype(BF16)
            k_ds[s, h] = (k * jnp.exp(last - cum)).astype(BF16)
            decays[s, h] = jnp.exp(last)
            vs_[s, h] = v_ref[sub, vsl[h]]
        atts = {w: _dot_nt(q_es[w], k_es[w]) for w in work}
        kvs = {w: lax.dot_general(vs_[w], k_ds[w], TN_DIMS, preferred_element_type=F32)
               for w in work}
        sts = {h: st_ref[h] for h in heads}
        inter = {}
        for s in range(n):
            for h in heads:
                inter[s, h] = _dot_nt(q_es[s, h], sts[h].astype(BF16))
            for h in heads:
                sts[h] = sts[h] * decays[s, h] + kvs[s, h]
        for h in heads:
            st_ref[h] = sts[h]
        atts = {w: jnp.where(causal, atts[w], 0.0).astype(BF16) for w in work}
        intra = {w: _dot(atts[w], vs_[w]) for w in work}
        for s, h in work:
            sub = pl.ds(base + s * c, c)
            gate = g_ref[sub, vsl[h]].astype(F32)
            o = intra[s, h] + inter[s, h]
            o_ref[sub, vsl[h]] = (_rms(o, gn) * (gate * jax.nn.sigmoid(gate))).astype(o_ref.dtype)
        return carry

    lax.fori_loop(0, nchunks // n, step, 0)


def _gla(u, wgk_pad, bgk, gnorm, batch, seq):
    m = batch * seq
    qk_wide = B_HEADS * B_DK
    v_wide = B_HEADS * B_DV
    return pl.pallas_call(
        functools.partial(_gla_kernel, nchunks=seq // CHUNK),
        grid=(batch,),
        in_specs=[
            pl.BlockSpec((seq, qk_wide), lambda b: (b, U_QB // qk_wide)),
            pl.BlockSpec((seq, qk_wide), lambda b: (b, U_KB // qk_wide)),
            pl.BlockSpec((seq, v_wide), lambda b: (b, U_VB // v_wide)),
            pl.BlockSpec((seq, v_wide), lambda b: (b, U_GB // v_wide)),
            pl.BlockSpec((seq, LANES), lambda b: (b, U_SMALL // LANES)),
            pl.BlockSpec((LANES, qk_wide), lambda b: (0, 0)),
            pl.BlockSpec((1, qk_wide), lambda b: (0, 0)),
            pl.BlockSpec((1, B_DV), lambda b: (0, 0)),
        ],
        out_specs=pl.BlockSpec((seq, v_wide), lambda b: (b, 0)),
        out_shape=jax.ShapeDtypeStruct((m, v_wide), BF16),
        scratch_shapes=[pltpu.VMEM((B_HEADS, B_DV, B_DK), F32)],
        compiler_params=_params("parallel"),
        name="gla",
    )(u, u, u, u, u, wgk_pad, bgk.reshape(1, -1), gnorm.reshape(1, -1))


def _sb_kernel(q_ref, k_ref, v_ref, o_ref, *, blk, nblk, head_dim, heads):
    t = blk
    d = head_dim
    row = lax.broadcasted_iota(I32, (t, t), 0)
    col = lax.broadcasted_iota(I32, (t, t), 1)
    upper = (row > col).astype(BF16)
    diag = col < row
    scale = d ** -0.5

    hsl = [slice(h * d, (h + 1) * d) for h in range(heads)]

    def block(qrows, krows, runs, mask):
        zs = [_dot_nt(q_ref[qrows, hs], k_ref[krows, hs]) * scale for hs in hsl]
        lms, log_bs = [], []
        for z in zs:
            log_1mb = -(jnp.maximum(z, 0.0) + jnp.log(1.0 + jnp.exp(-jnp.abs(z))))
            log_bs.append(log_1mb + z)
            lms.append(log_1mb if mask is None else jnp.where(mask, log_1mb, 0.0))
        sufs = [_dot(lm.astype(BF16), upper) for lm in lms]
        ws = []
        for suf, log_b, run in zip(sufs, log_bs, runs):
            w = jnp.exp(suf + run + log_b)
            ws.append((w if mask is None else jnp.where(mask, w, 0.0)).astype(BF16))
        outs = [_dot(w, v_ref[krows, hs]) for w, hs in zip(ws, hsl)]
        new_runs = [run + jnp.sum(lm, axis=1, keepdims=True) for run, lm in zip(runs, lms)]
        return new_runs, outs

    def q_block(qb, carry):
        qrows = pl.ds(pl.multiple_of(qb * t, t), t)
        runs, accs = block(qrows, qrows, [jnp.zeros((t, 1), F32)] * heads, diag)

        def alive(rs):
            top = jnp.max(rs[0])
            for r in rs[1:]:
                top = jnp.maximum(top, jnp.max(r))
            return (top >= SB_EXP_UNDERFLOW).astype(I32)

        def cond(state):
            i, live, _, _ = state
            return jnp.logical_and(i <= qb, live > 0)

        def k_block(state):
            i, _, rs, acs = state
            krows = pl.ds(pl.multiple_of((qb - i) * t, t), t)
            new_rs, outs = block(qrows, krows, list(rs), None)
            new_acs = [a + o for a, o in zip(acs, outs)]
            return i + 1, alive(new_rs), tuple(new_rs), tuple(new_acs)

        state = (jnp.int32(1), alive(runs), tuple(runs), tuple(accs))
        _, _, _, accs = lax.while_loop(cond, k_block, state)
        for h in range(heads):
            o_ref[qrows, h * d:(h + 1) * d] = accs[h].astype(o_ref.dtype)
        return carry

    lax.fori_loop(0, nblk, q_block, 0)


def _stick_breaking(qkv, batch, seq, blk):
    m = batch * seq
    d = qkv.shape[1] // (3 * C_HEADS)
    groups = C_HEADS // SB_HEADS_PER_STEP
    wide = SB_HEADS_PER_STEP * d
    return pl.pallas_call(
        functools.partial(_sb_kernel, blk=blk, nblk=seq // blk, head_dim=d, heads=SB_HEADS_PER_STEP),
        grid=(batch, groups),
        in_specs=[
            pl.BlockSpec((seq, wide), lambda b, g: (b, g)),
            pl.BlockSpec((seq, wide), lambda b, g: (b, groups + g)),
            pl.BlockSpec((seq, wide), lambda b, g: (b, 2 * groups + g)),
        ],
        out_specs=pl.BlockSpec((seq, wide), lambda b, g: (b, g)),
        out_shape=jax.ShapeDtypeStruct((m, C_HEADS * d), BF16),
        compiler_params=_params("parallel", "arbitrary"),
        name="stick_breaking",
    )(qkv, qkv, qkv)


_EVEN_WIDTHS = (A_HEADS * A_LATENT, A_LATENT, IDX_HEADS * IDX_DIM, IDX_DIM, IDX_HEADS,
                B_HEADS * B_DK, B_HEADS * B_DK, B_HEADS * B_DV, B_HEADS * B_DV, B_GATE_RANK)
_EVEN_STARTS = [sum(_EVEN_WIDTHS[:i]) for i in range(len(_EVEN_WIDTHS))]
_U_ORDER = (0, 2, 7, 8, 5, 6, 1, 3, 4, 9)
EVEN_SRC_COLS = tuple((_EVEN_STARTS[i], _EVEN_STARTS[i] + _EVEN_WIDTHS[i]) for i in _U_ORDER)


def _permute_w_in_kernel(w_ref, o_ref):
    w = w_ref[...]
    parts = [w[:, a:b] for a, b in EVEN_SRC_COLS]
    used = sum(b - a for a, b in EVEN_SRC_COLS)
    parts.append(jnp.zeros((w.shape[0], U_WIDTH - used), w.dtype))
    o_ref[...] = jnp.concatenate(parts, axis=1).astype(o_ref.dtype)


def _permute_even_w_in(w, rows):
    d, n = w.shape
    return pl.pallas_call(
        _permute_w_in_kernel,
        grid=(d // rows,),
        in_specs=[pl.BlockSpec((rows, n), lambda i: (i, 0))],
        out_specs=pl.BlockSpec((rows, U_WIDTH), lambda i: (i, 0)),
        out_shape=jax.ShapeDtypeStruct((d, U_WIDTH), BF16),
        compiler_params=_params("parallel"),
        name="permute_w_in",
    )(w)


def kernel(x, p, even_norm, even_w_in, even_w_gk, even_b_gk, even_gla_norm, even_w_out,
           odd_norm, odd_w_in, odd_w_out, ffn_norm, ffn_w_gate, ffn_w_up, ffn_w_down,
           ple_norm, ple_w_gate, ple_w_proj, final_norm):
    batch, seq, d = x.shape
    depth = p.shape[0]
    m = batch * seq
    h = x.reshape(m, d)
    tm = _tile(m, 1024)
    tm_row = _tile(m, 512)
    tn_pref = 1792
    blk = _tile(seq, 256)
    dsa_blk = _tile(seq, DSA_BLOCK)

    def mixer_norm(i):
        return even_norm[i // 2] if i % 2 == 0 else odd_norm[i // 2]

    def in_proj(h, hn, g, w, tn):
        return _norm_matmul(h, g, w, tm, tn) if hn is None else _matmul(hn, w, tm, tn)

    ffn_wg, ffn_wu, ffn_wd = (w.astype(BF16) for w in (ffn_w_gate, ffn_w_up, ffn_w_down))
    ple_wg, ple_wp = ple_w_gate.astype(BF16), ple_w_proj.astype(BF16)

    hn = None
    for i in range(depth):
        j = i // 2
        if i % 2 == 0:
            w_in = _permute_even_w_in(even_w_in[j], _tile(d, 256))
            u = in_proj(h, hn, even_norm[j], w_in, _tile(U_WIDTH, tn_pref))
            o_a = _dsa(u, batch, seq, dsa_blk, dsa_blk)
            wgk_pad = jnp.zeros((LANES, B_HEADS * B_DK), F32)
            wgk_pad = wgk_pad.at[SM_GK:SM_GK + B_GATE_RANK].set(even_w_gk[j]).astype(BF16)
            o_b = _gla(u, wgk_pad, even_b_gk[j], even_gla_norm[j], batch, seq)
            h, xn = _proj_residual(h, ffn_norm[i], [o_a, o_b], even_w_out[j].astype(BF16), tm_row)
        else:
            qkv = in_proj(h, hn, odd_norm[j], odd_w_in[j].astype(BF16), _tile(3 * d, tn_pref))
            o = _stick_breaking(qkv, batch, seq, blk)
            h, xn = _proj_residual(h, ffn_norm[i], [o], odd_w_out[j].astype(BF16), tm_row)
        y = _ffn(xn, ffn_wg, ffn_wu, ffn_wd, i, tm, _tile(ffn_wg.shape[2], 512))
        last = i == depth - 1
        out = _ple(h, y, ple_norm[i], ple_wg, p.reshape(depth, m, -1), i, ple_wp,
                   final_norm if last else mixer_norm(i + 1), tm_row, last)
        h, hn = (out, None) if last else out
    return h.reshape(batch, seq, d)
```

```python
import functools

import jax
import jax.numpy as jnp
from jax import lax
from jax.experimental import pallas as pl
from jax.experimental.pallas import tpu as pltpu

F32 = jnp.float32
BF16 = jnp.bfloat16
I32 = jnp.int32

EPS = 1e-6
CHUNK = 64
CHUNK_SHIFT = CHUNK.bit_length() - 1
A_HEADS = 8
A_LATENT = 128
A_TOPK_MAX = 256
IDX_HEADS = 16
IDX_DIM = 64
B_HEADS = 4
B_DK = 128
B_DV = 256
B_GATE_RANK = 16
B_GATE_NORM = 16.0
DSA_BLOCK = 512
GLA_CHUNKS_PER_STEP = 4
C_HEADS = 16
LOG2_E = 1.4426950408889634
SB_HEADS_PER_STEP = 4
SB_EXP_UNDERFLOW = -104.0

LANES = 128
SUBLANES = 8
VMEM_LIMIT_BYTES = 56 * 1024 * 1024

U_QA = 0
U_QI = 1024
U_VB = 2048
U_GB = 3072
U_QB = 4096
U_KB = 4608
U_CA = 5120
U_SMALL = 5248
U_WIDTH = 5376
SM_W = 64
SM_GK = 80

NT_DIMS = (((1,), (1,)), ((), ()))
TN_DIMS = (((0,), (0,)), ((), ()))

KEY_NEG_INF = -2139095041
INT_MIN = -2147483648


def _params(*sem):
    return pltpu.CompilerParams(dimension_semantics=sem, vmem_limit_bytes=VMEM_LIMIT_BYTES)


def _tile(n, pref):
    if n <= pref:
        return n
    t = (pref // LANES) * LANES
    while n % t:
        t -= LANES
    return t


def _rms(x, g):
    ms = jnp.mean(x * x, axis=-1, keepdims=True)
    return x * lax.rsqrt(ms + EPS) * g


def _dot(a, b):
    return jnp.dot(a, b, preferred_element_type=F32)


def _dot_nt(a, b):
    return lax.dot_general(a, b, NT_DIMS, preferred_element_type=F32)


def _softplus_tail(z):
    return jnp.log1p(jnp.exp(-jnp.abs(z)))


def _split_bf16(x):
    hi = x.astype(BF16)
    lo = (x - hi.astype(F32)).astype(BF16)
    return hi, lo


def _norm_matmul_kernel(x_ref, g_ref, w_ref, o_ref, xn_ref):
    @pl.when(pl.program_id(1) == 0)
    def _():
        xn_ref[...] = _rms(x_ref[...], g_ref[...]).astype(BF16)

    o_ref[...] = _dot(xn_ref[...], w_ref[...]).astype(o_ref.dtype)


def _norm_matmul(x, g, w, tm, tn):
    m, k = x.shape
    n = w.shape[1]
    return pl.pallas_call(
        _norm_matmul_kernel,
        grid=(m // tm, n // tn),
        in_specs=[
            pl.BlockSpec((tm, k), lambda i, j: (i, 0)),
            pl.BlockSpec((1, k), lambda i, j: (0, 0)),
            pl.BlockSpec((k, tn), lambda i, j: (0, j)),
        ],
        out_specs=pl.BlockSpec((tm, tn), lambda i, j: (i, j)),
        out_shape=jax.ShapeDtypeStruct((m, n), BF16),
        scratch_shapes=[pltpu.VMEM((tm, k), BF16)],
        compiler_params=_params("parallel", "arbitrary"),
        name="norm_matmul",
    )(x, g.reshape(1, k), w)


def _matmul_kernel(x_ref, w_ref, o_ref):
    o_ref[...] = _dot(x_ref[...], w_ref[...]).astype(o_ref.dtype)


def _matmul(xn, w, tm, tn):
    m, k = xn.shape
    n = w.shape[1]
    return pl.pallas_call(
        _matmul_kernel,
        grid=(m // tm, n // tn),
        in_specs=[
            pl.BlockSpec((tm, k), lambda i, j: (i, 0)),
            pl.BlockSpec((k, tn), lambda i, j: (0, j)),
        ],
        out_specs=pl.BlockSpec((tm, tn), lambda i, j: (i, j)),
        out_shape=jax.ShapeDtypeStruct((m, n), BF16),
        compiler_params=_params("parallel", "arbitrary"),
        name="matmul",
    )(xn, w)


def _proj_residual_kernel(*refs, n_in):
    h_ref, g_ref, w_ref = refs[:3]
    a_refs = refs[3:3 + n_in]
    o_ref, xn_ref = refs[3 + n_in:]
    acc = h_ref[...]
    row = 0
    for a_ref in a_refs:
        k = a_ref.shape[1]
        acc = acc + _dot(a_ref[...], w_ref[row:row + k, :])
        row += k
    o_ref[...] = acc
    xn_ref[...] = _rms(acc, g_ref[...]).astype(BF16)


def _resident(shape):
    return pl.BlockSpec(shape, lambda i: (0,) * len(shape), pipeline_mode=pl.Buffered(1))


def _proj_residual(h, g, a_list, w, tm):
    m, n = h.shape
    assert sum(a.shape[1] for a in a_list) == w.shape[0]
    row_block = pl.BlockSpec((tm, n), lambda i: (i, 0))
    in_specs = [row_block, _resident((1, n)), _resident(w.shape)]
    in_specs += [pl.BlockSpec((tm, a.shape[1]), lambda i: (i, 0)) for a in a_list]
    return pl.pallas_call(
        functools.partial(_proj_residual_kernel, n_in=len(a_list)),
        grid=(m // tm,),
        in_specs=in_specs,
        out_specs=[row_block, row_block],
        out_shape=[jax.ShapeDtypeStruct((m, n), F32), jax.ShapeDtypeStruct((m, n), BF16)],
        compiler_params=_params("parallel"),
        name="proj_residual",
    )(h, g.reshape(1, n), w, *a_list)


def _ffn_kernel(xn_ref, wg_ref, wu_ref, wd_ref, o_ref):
    @pl.when(pl.program_id(1) == 0)
    def _():
        o_ref[...] = jnp.zeros(o_ref.shape, F32)

    xn = xn_ref[...]
    gate = _dot(xn, wg_ref[...])
    up = _dot(xn, wu_ref[...])
    act = (gate * jax.nn.sigmoid(gate) * up).astype(BF16)
    o_ref[...] += _dot(act, wd_ref[...])


def _ffn(xn, wg, wu, wd, layer, tm, th):
    m, d = xn.shape
    hid = wg.shape[2]
    return pl.pallas_call(
        _ffn_kernel,
        grid=(m // tm, hid // th),
        in_specs=[
            pl.BlockSpec((tm, d), lambda i, j: (i, 0)),
            pl.BlockSpec((None, d, th), lambda i, j: (layer, 0, j)),
            pl.BlockSpec((None, d, th), lambda i, j: (layer, 0, j)),
            pl.BlockSpec((None, th, d), lambda i, j: (layer, j, 0)),
        ],
        out_specs=pl.BlockSpec((tm, d), lambda i, j: (i, 0)),
        out_shape=jax.ShapeDtypeStruct((m, d), F32),
        compiler_params=_params("parallel", "arbitrary"),
        name="ffn",
    )(xn, wg, wu, wd)


def _ple_kernel(x_ref, y_ref, g_ref, wg_ref, p_ref, wp_ref, ng_ref, *o_refs, last):
    x = x_ref[...] + y_ref[...]
    gate = jax.nn.sigmoid(_dot(_rms(x, g_ref[...]).astype(BF16), wg_ref[...]))
    out = x + _dot(p_ref[...].astype(BF16), wp_ref[...]) * gate
    if last:
        o_refs[0][...] = _rms(out, ng_ref[...])
    else:
        o_refs[0][...] = out
        o_refs[1][...] = _rms(out, ng_ref[...]).astype(BF16)


def _ple(h, y, g, wg, p, layer, wp, next_g, tm, last):
    m, d = h.shape
    pd = p.shape[2]
    row_block = pl.BlockSpec((tm, d), lambda i: (i, 0))
    h_shape = jax.ShapeDtypeStruct((m, d), F32)

    def resident_layer(rows, cols):
        return pl.BlockSpec((None, rows, cols), lambda i: (layer, 0, 0),
                            pipeline_mode=pl.Buffered(1))

    return pl.pallas_call(
        functools.partial(_ple_kernel, last=last),
        grid=(m // tm,),
        in_specs=[
            row_block,
            row_block,
            _resident((1, d)),
            resident_layer(d, d),
            pl.BlockSpec((None, tm, pd), lambda i: (layer, i, 0)),
            resident_layer(pd, d),
            _resident((1, d)),
        ],
        out_specs=row_block if last else [row_block, row_block],
        out_shape=h_shape if last else [h_shape, jax.ShapeDtypeStruct((m, d), BF16)],
        compiler_params=_params("parallel"),
        name="ple",
    )(h, y, g.reshape(1, d), wg, p, wp, next_g.reshape(1, d))


def _key_to_float(key):
    bits = key ^ ((key >> 31) & 0x7FFFFFFF)
    return lax.bitcast_convert_type(bits, F32)


def _dsa_kernel(qi_ref, qsm_ref, ksm_ref, qa_ref, c_ref, o_ref,
                sc_ref, scb_ref, ct_ref, acc_ref, m_ref, l_ref, *, topk, qb_size, key_tile, seq):
    tq, tk = qb_size, key_tile
    qb = pl.program_id(1)
    nk = (qb + 1) * (tq // tk)

    @pl.when(qb == 0)
    def _():
        ct_ref[...] = jnp.transpose(c_ref[...].astype(F32)).astype(BF16)

    row = lax.broadcasted_iota(I32, (tk, tq), 0)
    col = lax.broadcasted_iota(I32, (tk, tq), 1)
    limit = (((qb * tq + col) >> CHUNK_SHIFT) + 1) << CHUNK_SHIFT

    w_t = jnp.transpose(qsm_ref[...].astype(F32))[SM_W:SM_W + IDX_HEADS, :]

    def score_tile(kt, carry):
        rows = pl.ds(pl.multiple_of(kt * tk, tk), tk)
        kk = ksm_ref[rows, :][:, :IDX_DIM]
        sc = jnp.zeros((tk, tq), F32)
        for h in range(IDX_HEADS):
            qh = qi_ref[:, h * IDX_DIM:(h + 1) * IDX_DIM]
            sc = sc + jnp.maximum(_dot_nt(kk, qh), 0.0) * w_t[h:h + 1, :]
        sc = sc * (IDX_DIM ** -0.5 * IDX_HEADS ** -0.5)
        sc = jnp.where(kt * tk + row < limit, sc, -jnp.inf)
        sc_ref[rows, :] = sc
        scb_ref[rows, :] = sc.astype(BF16)
        return carry

    lax.fori_loop(0, nk, score_tile, 0)

    def count(pred):
        def body(kt, acc):
            rows = pl.ds(pl.multiple_of(kt * tk, tk), tk)
            hit = pred(sc_ref[rows, :]).astype(I32)
            return acc + jnp.sum(hit.reshape(tk // SUBLANES, SUBLANES, tq), axis=0)
        acc = lax.fori_loop(0, nk, body, jnp.zeros((SUBLANES, tq), I32))
        return jnp.sum(acc, axis=0, keepdims=True)

    def count_ge(key):
        cand = _key_to_float(key)
        return count(lambda s: s >= cand)

    pack = 2 * SUBLANES

    def count_ge_coarse(key):
        bits = (key ^ ((key >> 31) & 0x7FFFFFFF)) & -65536
        cand = lax.bitcast_convert_type(bits, F32).astype(BF16)

        def body(kt, acc):
            rows = pl.ds(pl.multiple_of(kt * tk, tk), tk)
            hit = jnp.where(scb_ref[rows, :] >= cand, jnp.ones((), BF16), jnp.zeros((), BF16))
            part = hit[0:pack, :]
            for r in range(1, tk // pack):
                part = part + hit[r * pack:(r + 1) * pack, :]
            return acc + part.astype(F32)

        acc = lax.fori_loop(0, nk, body, jnp.zeros((pack, tq), F32))
        return jnp.sum(acc, axis=0, keepdims=True)

    zero_key = jnp.zeros((1, tq), I32)
    res0 = jnp.where(count_ge_coarse(zero_key) >= topk, 0, INT_MIN).astype(I32)

    def bisect_coarse(i, res):
        cand = res | jnp.left_shift(jnp.int32(1), 30 - i)
        return jnp.where(count_ge_coarse(cand) >= topk, cand, res)

    coarse = lax.fori_loop(0, 15, bisect_coarse, res0)
    coarse = jnp.maximum(coarse, KEY_NEG_INF - 65535)
    center = jnp.where(coarse < 0, coarse + 65535, coarse)
    half_step = 1 << 15
    lo = center - (half_step + 1)

    def bisect_fine(i, res):
        cand = res + jnp.left_shift(jnp.int32(1), 16 - i)
        return jnp.where(count_ge(cand) >= topk, cand, res)

    res = lax.fori_loop(0, 17, bisect_fine, lo)
    thr = _key_to_float(jnp.maximum(res, KEY_NEG_INF))
    n_ge = count(lambda s: s >= thr) + jnp.where(thr == -jnp.inf, seq - nk * tk, 0)

    @pl.when(jnp.max(n_ge) > topk)
    def _():
        need = (topk - count(lambda s: s > thr)).astype(F32)
        tri = (lax.broadcasted_iota(I32, (tk, tk), 0)
               >= lax.broadcasted_iota(I32, (tk, tk), 1)).astype(BF16)

        def body(kt, seen):
            rows = pl.ds(pl.multiple_of(kt * tk, tk), tk)
            s = sc_ref[rows, :]
            eq = s == thr
            rank = _dot(tri, eq.astype(BF16)) + seen
            drop = (eq & (rank > need)) | (kt * tk + row >= limit)
            sc_ref[rows, :] = jnp.where(drop, jnp.nan, s)
            return seen + jnp.sum(eq.astype(F32), axis=0, keepdims=True)

        lax.fori_loop(0, nk, body, jnp.zeros((1, tq), F32))

    m_ref[...] = jnp.full(m_ref.shape, -1e30, F32)
    l_ref[...] = jnp.zeros(l_ref.shape, F32)
    acc_ref[...] = jnp.zeros(acc_ref.shape, F32)

    heads = range(A_HEADS)
    hsl = [slice(h * A_LATENT, (h + 1) * A_LATENT) for h in heads]
    c2 = (A_LATENT ** -0.5) * LOG2_E

    def attend(kt, carry):
        rows = pl.ds(pl.multiple_of(kt * tk, tk), tk)
        valid = sc_ref[rows, :] >= thr
        ck = c_ref[rows, :]
        ckt = ct_ref[:, rows]
        raws = [jnp.where(valid, _dot_nt(ck, qa_ref[:, hs]), -1e30) for hs in hsl]
        ps, alphas = [], []
        for h, raw in zip(heads, raws):
            m_old = m_ref[h:h + 1, :]
            m_new = jnp.maximum(m_old, jnp.max(raw, axis=0, keepdims=True))
            alpha = jnp.exp2((m_old - m_new) * c2)
            p = jnp.exp2((raw - m_new) * c2)
            l_ref[h:h + 1, :] = alpha * l_ref[h:h + 1, :] + jnp.sum(p, axis=0, keepdims=True)
            m_ref[h:h + 1, :] = m_new
            ps.append(p.astype(BF16))
            alphas.append(alpha)
        pvs = [_dot(ckt, p) for p in ps]
        for hs, alpha, pv in zip(hsl, alphas, pvs):
            acc_ref[hs, :] = alpha * acc_ref[hs, :] + pv
        return carry

    lax.fori_loop(0, nk, attend, 0)

    for h in heads:
        out = acc_ref[hsl[h], :] / l_ref[h:h + 1, :]
        o_ref[:, hsl[h]] = jnp.transpose(out).astype(o_ref.dtype)


def _dsa(u, batch, seq, qb_size, key_tile):
    m = batch * seq
    nqb = seq // qb_size
    assert qb_size % key_tile == 0 and key_tile % CHUNK == 0
    topk = min(A_TOPK_MAX, seq // 4)
    wide = A_HEADS * A_LATENT
    return pl.pallas_call(
        functools.partial(_dsa_kernel, topk=topk, qb_size=qb_size, key_tile=key_tile, seq=seq),
        grid=(batch, nqb),
        in_specs=[
            pl.BlockSpec((qb_size, wide), lambda b, q: (b * nqb + q, U_QI // wide)),
            pl.BlockSpec((qb_size, LANES), lambda b, q: (b * nqb + q, U_SMALL // LANES)),
            pl.BlockSpec((seq, LANES), lambda b, q: (b, U_SMALL // LANES)),
            pl.BlockSpec((qb_size, wide), lambda b, q: (b * nqb + q, U_QA // wide)),
            pl.BlockSpec((seq, LANES), lambda b, q: (b, U_CA // LANES)),
        ],
        out_specs=pl.BlockSpec((qb_size, wide), lambda b, q: (b * nqb + q, 0)),
        out_shape=jax.ShapeDtypeStruct((m, wide), BF16),
        scratch_shapes=[
            pltpu.VMEM((seq, qb_size), F32),
            pltpu.VMEM((seq, qb_size), BF16),
            pltpu.VMEM((A_LATENT, seq), BF16),
            pltpu.VMEM((wide, qb_size), F32),
            pltpu.VMEM((A_HEADS, qb_size), F32),
            pltpu.VMEM((A_HEADS, qb_size), F32),
        ],
        compiler_params=_params("parallel", "arbitrary"),
        name="dsa",
    )(u, u, u, u, u)


def _gla_kernel(q_ref, k_ref, v_ref, g_ref, sm_ref, wgk_ref, bgk_ref, gn_ref, o_ref, st_ref,
                *, nchunks):
    c = CHUNK
    n = GLA_CHUNKS_PER_STEP
    span = n * c
    row = lax.broadcasted_iota(I32, (span, span), 0)
    col = lax.broadcasted_iota(I32, (span, span), 1)
    tri = ((row >= col) & ((row >> CHUNK_SHIFT) == (col >> CHUNK_SHIFT))).astype(BF16)
    causal = lax.broadcasted_iota(I32, (c, c), 0) >= lax.broadcasted_iota(I32, (c, c), 1)
    st_ref[...] = jnp.zeros(st_ref.shape, F32)
    wgk = wgk_ref[...]
    bgk = bgk_ref[...]
    gn = gn_ref[...]
    heads = range(B_HEADS)
    ksl = [slice(h * B_DK, (h + 1) * B_DK) for h in heads]
    vsl = [slice(h * B_DV, (h + 1) * B_DV) for h in heads]
    work = [(s, h) for s in range(n) for h in heads]

    def step(si, carry):
        base = pl.multiple_of(si * span, span)
        rows = pl.ds(base, span)
        x = _dot(sm_ref[rows, :], wgk) + bgk
        gk = -(jnp.maximum(-x, 0.0) + _softplus_tail(x)) / B_GATE_NORM
        hi, lo = _split_bf16(gk)
        cum_all = _dot(tri, hi) + _dot(tri, lo)
        q_es, k_es, k_ds, decays, vs_ = {}, {}, {}, {}, {}
        for s, h in work:
            sub = pl.ds(base + s * c, c)
            cum = cum_all[s * c:(s + 1) * c, ksl[h]]
            last = cum[c - 1:c, :]
            q = q_ref[sub, ksl[h]].astype(F32) * (B_DK ** -0.5)
            k = k_ref[sub, ksl[h]].astype(F32)
            q_es[s, h] = (q * jnp.exp(cum)).astype(BF16)
            k_es[s, h] = (k * jnp.exp(-cum)).astype(BF16)
            k_ds[s, h] = (k * jnp.exp(last - cum)).astype(BF16)
            decays[s, h] = jnp.exp(last)
            vs_[s, h] = v_ref[sub, vsl[h]]
        atts = {w: _dot_nt(q_es[w], k_es[w]) for w in work}
        kvs = {w: lax.dot_general(vs_[w], k_ds[w], TN_DIMS, preferred_element_type=F32)
               for w in work}
        sts = {h: st_ref[h] for h in heads}
        inter = {}
        for s in range(n):
            for h in heads:
                inter[s, h] = _dot_nt(q_es[s, h], sts[h].astype(BF16))
            for h in heads:
                sts[h] = sts[h] * decays[s, h] + kvs[s, h]
        for h in heads:
            st_ref[h] = sts[h]
        atts = {w: jnp.where(causal, atts[w], 0.0).astype(BF16) for w in work}
        intra = {w: _dot(atts[w], vs_[w]) for w in work}
        for s, h in work:
            sub = pl.ds(base + s * c, c)
            gate = g_ref[sub, vsl[h]].astype(F32)
            o = intra[s, h] + inter[s, h]
            o_ref[sub, vsl[h]] = (_rms(o, gn) * (gate * jax.nn.sigmoid(gate))).astype(o_ref.dtype)
        return carry

    lax.fori_loop(0, nchunks // n, step, 0)


def _gla(u, wgk_pad, bgk, gnorm, batch, seq):
    m = batch * seq
    qk_wide = B_HEADS * B_DK
    v_wide = B_HEADS * B_DV
    return pl.pallas_call(
        functools.partial(_gla_kernel, nchunks=seq // CHUNK),
        grid=(batch,),
        in_specs=[
            pl.BlockSpec((seq, qk_wide), lambda b: (b, U_QB // qk_wide)),
            pl.BlockSpec((seq, qk_wide), lambda b: (b, U_KB // qk_wide)),
            pl.BlockSpec((seq, v_wide), lambda b: (b, U_VB // v_wide)),
            pl.BlockSpec((seq, v_wide), lambda b: (b, U_GB // v_wide)),
            pl.BlockSpec((seq, LANES), lambda b: (b, U_SMALL // LANES)),
            pl.BlockSpec((LANES, qk_wide), lambda b: (0, 0)),
            pl.BlockSpec((1, qk_wide), lambda b: (0, 0)),
            pl.BlockSpec((1, B_DV), lambda b: (0, 0)),
        ],
        out_specs=pl.BlockSpec((seq, v_wide), lambda b: (b, 0)),
        out_shape=jax.ShapeDtypeStruct((m, v_wide), BF16),
        scratch_shapes=[pltpu.VMEM((B_HEADS, B_DV, B_DK), F32)],
        compiler_params=_params("parallel"),
        name="gla",
    )(u, u, u, u, u, wgk_pad, bgk.reshape(1, -1), gnorm.reshape(1, -1))


def _sb_kernel(q_ref, k_ref, v_ref, o_ref, *, blk, nblk, head_dim, heads):
    t = blk
    d = head_dim
    row = lax.broadcasted_iota(I32, (t, t), 0)
    col = lax.broadcasted_iota(I32, (t, t), 1)
    upper = (row > col).astype(BF16)
    diag = col < row
    scale = d ** -0.5

    hsl = [slice(h * d, (h + 1) * d) for h in range(heads)]

    def block(qrows, krows, runs, mask):
        zs = [_dot_nt(q_ref[qrows, hs], k_ref[krows, hs]) * scale for hs in hsl]
        lms, log_bs = [], []
        for z in zs:
            log_1mb = -(jnp.maximum(z, 0.0) + jnp.log(1.0 + jnp.exp(-jnp.abs(z))))
            log_bs.append(log_1mb + z)
            lms.append(log_1mb if mask is None else jnp.where(mask, log_1mb, 0.0))
        sufs = [_dot(lm.astype(BF16), upper) for lm in lms]
        ws = []
        for suf, log_b, run in zip(sufs, log_bs, runs):
            w = jnp.exp(suf + run + log_b)
            ws.append((w if mask is None else jnp.where(mask, w, 0.0)).astype(BF16))
        outs = [_dot(w, v_ref[krows, hs]) for w, hs in zip(ws, hsl)]
        new_runs = [run + jnp.sum(lm, axis=1, keepdims=True) for run, lm in zip(runs, lms)]
        return new_runs, outs

    def q_block(qb, carry):
        qrows = pl.ds(pl.multiple_of(qb * t, t), t)
        runs, accs = block(qrows, qrows, [jnp.zeros((t, 1), F32)] * heads, diag)

        def alive(rs):
            top = jnp.max(rs[0])
            for r in rs[1:]:
                top = jnp.maximum(top, jnp.max(r))
            return (top >= SB_EXP_UNDERFLOW).astype(I32)

        def cond(state):
            i, live, _, _ = state
            return jnp.logical_and(i <= qb, live > 0)

        def k_block(state):
            i, _, rs, acs = state
            krows = pl.ds(pl.multiple_of((qb - i) * t, t), t)
            new_rs, outs = block(qrows, krows, list(rs), None)
            new_acs = [a + o for a, o in zip(acs, outs)]
            return i + 1, alive(new_rs), tuple(new_rs), tuple(new_acs)

        state = (jnp.int32(1), alive(runs), tuple(runs), tuple(accs))
        _, _, _, accs = lax.while_loop(cond, k_block, state)
        for h in range(heads):
            o_ref[qrows, h * d:(h + 1) * d] = accs[h].astype(o_ref.dtype)
        return carry

    lax.fori_loop(0, nblk, q_block, 0)


def _stick_breaking(qkv, batch, seq, blk):
    m = batch * seq
    d = qkv.shape[1] // (3 * C_HEADS)
    groups = C_HEADS // SB_HEADS_PER_STEP
    wide = SB_HEADS_PER_STEP * d
    return pl.pallas_call(
        functools.partial(_sb_kernel, blk=blk, nblk=seq // blk, head_dim=d, heads=SB_HEADS_PER_STEP),
        grid=(batch, groups),
        in_specs=[
            pl.BlockSpec((seq, wide), lambda b, g: (b, g)),
            pl.BlockSpec((seq, wide), lambda b, g: (b, groups + g)),
            pl.BlockSpec((seq, wide), lambda b, g: (b, 2 * groups + g)),
        ],
        out_specs=pl.BlockSpec((seq, wide), lambda b, g: (b, g)),
        out_shape=jax.ShapeDtypeStruct((m, C_HEADS * d), BF16),
        compiler_params=_params("parallel", "arbitrary"),
        name="stick_breaking",
    )(qkv, qkv, qkv)


_EVEN_WIDTHS = (A_HEADS * A_LATENT, A_LATENT, IDX_HEADS * IDX_DIM, IDX_DIM, IDX_HEADS,
                B_HEADS * B_DK, B_HEADS * B_DK, B_HEADS * B_DV, B_HEADS * B_DV, B_GATE_RANK)
_EVEN_STARTS = [sum(_EVEN_WIDTHS[:i]) for i in range(len(_EVEN_WIDTHS))]
_U_ORDER = (0, 2, 7, 8, 5, 6, 1, 3, 4, 9)
EVEN_SRC_COLS = tuple((_EVEN_STARTS[i], _EVEN_STARTS[i] + _EVEN_WIDTHS[i]) for i in _U_ORDER)


def _permute_w_in_kernel(w_ref, o_ref):
    w = w_ref[...]
    parts = [w[:, a:b] for a, b in EVEN_SRC_COLS]
    used = sum(b - a for a, b in EVEN_SRC_COLS)
    parts.append(jnp.zeros((w.shape[0], U_WIDTH - used), w.dtype))
    o_ref[...] = jnp.concatenate(parts, axis=1).astype(o_ref.dtype)


def _permute_even_w_in(w, rows):
    d, n = w.shape
    return pl.pallas_call(
        _permute_w_in_kernel,
        grid=(d // rows,),
        in_specs=[pl.BlockSpec((rows, n), lambda i: (i, 0))],
        out_specs=pl.BlockSpec((rows, U_WIDTH), lambda i: (i, 0)),
        out_shape=jax.ShapeDtypeStruct((d, U_WIDTH), BF16),
        compiler_params=_params("parallel"),
        name="permute_w_in",
    )(w)


def kernel(x, p, even_norm, even_w_in, even_w_gk, even_b_gk, even_gla_norm, even_w_out,
           odd_norm, odd_w_in, odd_w_out, ffn_norm, ffn_w_gate, ffn_w_up, ffn_w_down,
           ple_norm, ple_w_gate, ple_w_proj, final_norm):
    batch, seq, d = x.shape
    depth = p.shape[0]
    m = batch * seq
    h = x.reshape(m, d)
    tm = _tile(m, 1024)
    tm_row = _tile(m, 512)
    tn_pref = 1792
    blk = _tile(seq, 256)
    dsa_blk = _tile(seq, DSA_BLOCK)

    def mixer_norm(i):
        return even_norm[i // 2] if i % 2 == 0 else odd_norm[i // 2]

    def in_proj(h, hn, g, w, tn):
        return _norm_matmul(h, g, w, tm, tn) if hn is None else _matmul(hn, w, tm, tn)

    ffn_wg, ffn_wu, ffn_wd = (w.astype(BF16) for w in (ffn_w_gate, ffn_w_up, ffn_w_down))
    ple_wg, ple_wp = ple_w_gate.astype(BF16), ple_w_proj.astype(BF16)

    hn = None
    for i in range(depth):
        j = i // 2
        if i % 2 == 0:
            w_in = _permute_even_w_in(even_w_in[j], _tile(d, 256))
            u = in_proj(h, hn, even_norm[j], w_in, _tile(U_WIDTH, tn_pref))
            o_a = _dsa(u, batch, seq, dsa_blk, dsa_blk)
            wgk_pad = jnp.zeros((LANES, B_HEADS * B_DK), F32)
            wgk_pad = wgk_pad.at[SM_GK:SM_GK + B_GATE_RANK].set(even_w_gk[j]).astype(BF16)
            o_b = _gla(u, wgk_pad, even_b_gk[j], even_gla_norm[j], batch, seq)
            h, xn = _proj_residual(h, ffn_norm[i], [o_a, o_b], even_w_out[j].astype(BF16), tm_row)
        else:
            qkv = in_proj(h, hn, odd_norm[j], odd_w_in[j].astype(BF16), _tile(3 * d, tn_pref))
            o = _stick_breaking(qkv, batch, seq, blk)
            h, xn = _proj_residual(h, ffn_norm[i], [o], odd_w_out[j].astype(BF16), tm_row)
        y = _ffn(xn, ffn_wg, ffn_wu, ffn_wd, i, tm, _tile(ffn_wg.shape[2], 512))
        last = i == depth - 1
        out = _ple(h, y, ple_norm[i], ple_wg, p.reshape(depth, m, -1), i, ple_wp,
                   final_norm if last else mixer_norm(i + 1), tm_row, last)
        h, hn = (out, None) if last else out
    return h.reshape(batch, seq, d)
```

```python
import functools

import jax
import jax.numpy as jnp
from jax import lax
from jax.experimental import pallas as pl
from jax.experimental.pallas import tpu as pltpu

F32 = jnp.float32
BF16 = jnp.bfloat16
I32 = jnp.int32

EPS = 1e-6
CHUNK = 64
CHUNK_SHIFT = CHUNK.bit_length() - 1
A_HEADS = 8
A_LATENT = 128
A_TOPK_MAX = 256
IDX_HEADS = 16
IDX_DIM = 64
B_HEADS = 4
B_DK = 128
B_DV = 256
B_GATE_RANK = 16
B_GATE_NORM = 16.0
DSA_BLOCK = 512
GLA_CHUNKS_PER_STEP = 4
C_HEADS = 16
LOG2_E = 1.4426950408889634
SB_HEADS_PER_STEP = 4
SB_EXP_UNDERFLOW = -104.0

LANES = 128
SUBLANES = 8
VMEM_LIMIT_BYTES = 56 * 1024 * 1024

U_QA = 0
U_QI = 1024
U_VB = 2048
U_GB = 3072
U_QB = 4096
U_KB = 4608
U_CA = 5120
U_SMALL = 5248
U_WIDTH = 5376
SM_W = 64
SM_GK = 80

NT_DIMS = (((1,), (1,)), ((), ()))
TN_DIMS = (((0,), (0,)), ((), ()))

KEY_NEG_INF = -2139095041
BF16_DROPPED_BITS = 16
INT_MIN = -2147483648


def _params(*sem):
    return pltpu.CompilerParams(dimension_semantics=sem, vmem_limit_bytes=VMEM_LIMIT_BYTES)


def _tile(n, pref):
    if n <= pref:
        return n
    t = (pref // LANES) * LANES
    while n % t:
        t -= LANES
    return t


def _rms(x, g):
    ms = jnp.mean(x * x, axis=-1, keepdims=True)
    return x * lax.rsqrt(ms + EPS) * g


def _dot(a, b):
    return jnp.dot(a, b, preferred_element_type=F32)


def _dot_nt(a, b):
    return lax.dot_general(a, b, NT_DIMS, preferred_element_type=F32)


def _softplus_tail(z):
    return jnp.log1p(jnp.exp(-jnp.abs(z)))


def _split_bf16(x):
    hi = x.astype(BF16)
    lo = (x - hi.astype(F32)).astype(BF16)
    return hi, lo


def _norm_matmul_kernel(x_ref, g_ref, w_ref, o_ref, xn_ref):
    @pl.when(pl.program_id(1) == 0)
    def _():
        xn_ref[...] = _rms(x_ref[...], g_ref[...]).astype(BF16)

    o_ref[...] = _dot(xn_ref[...], w_ref[...]).astype(o_ref.dtype)


def _norm_matmul(x, g, w, tm, tn):
    m, k = x.shape
    n = w.shape[1]
    return pl.pallas_call(
        _norm_matmul_kernel,
        grid=(m // tm, n // tn),
        in_specs=[
            pl.BlockSpec((tm, k), lambda i, j: (i, 0)),
            pl.BlockSpec((1, k), lambda i, j: (0, 0)),
            pl.BlockSpec((k, tn), lambda i, j: (0, j)),
        ],
        out_specs=pl.BlockSpec((tm, tn), lambda i, j: (i, j)),
        out_shape=jax.ShapeDtypeStruct((m, n), BF16),
        scratch_shapes=[pltpu.VMEM((tm, k), BF16)],
        compiler_params=_params("parallel", "arbitrary"),
        name="norm_matmul",
    )(x, g.reshape(1, k), w)


def _matmul_kernel(x_ref, w_ref, o_ref):
    o_ref[...] = _dot(x_ref[...], w_ref[...]).astype(o_ref.dtype)


def _matmul(xn, w, tm, tn):
    m, k = xn.shape
    n = w.shape[1]
    return pl.pallas_call(
        _matmul_kernel,
        grid=(m // tm, n // tn),
        in_specs=[
            pl.BlockSpec((tm, k), lambda i, j: (i, 0)),
            pl.BlockSpec((k, tn), lambda i, j: (0, j)),
        ],
        out_specs=pl.BlockSpec((tm, tn), lambda i, j: (i, j)),
        out_shape=jax.ShapeDtypeStruct((m, n), BF16),
        compiler_params=_params("parallel", "arbitrary"),
        name="matmul",
    )(xn, w)


def _proj_residual_kernel(*refs, n_in):
    h_ref, g_ref, w_ref = refs[:3]
    a_refs = refs[3:3 + n_in]
    o_ref, xn_ref = refs[3 + n_in:]
    acc = h_ref[...]
    row = 0
    for a_ref in a_refs:
        k = a_ref.shape[1]
        acc = acc + _dot(a_ref[...], w_ref[row:row + k, :])
        row += k
    o_ref[...] = acc
    xn_ref[...] = _rms(acc, g_ref[...]).astype(BF16)


def _resident(shape):
    return pl.BlockSpec(shape, lambda i: (0,) * len(shape), pipeline_mode=pl.Buffered(1))


def _proj_residual(h, g, a_list, w, tm):
    m, n = h.shape
    assert sum(a.shape[1] for a in a_list) == w.shape[0]
    row_block = pl.BlockSpec((tm, n), lambda i: (i, 0))
    in_specs = [row_block, _resident((1, n)), _resident(w.shape)]
    in_specs += [pl.BlockSpec((tm, a.shape[1]), lambda i: (i, 0)) for a in a_list]
    return pl.pallas_call(
        functools.partial(_proj_residual_kernel, n_in=len(a_list)),
        grid=(m // tm,),
        in_specs=in_specs,
        out_specs=[row_block, row_block],
        out_shape=[jax.ShapeDtypeStruct((m, n), F32), jax.ShapeDtypeStruct((m, n), BF16)],
        compiler_params=_params("parallel"),
        name="proj_residual",
    )(h, g.reshape(1, n), w, *a_list)


def _ffn_kernel(xn_ref, wg_ref, wu_ref, wd_ref, o_ref):
    @pl.when(pl.program_id(1) == 0)
    def _():
        o_ref[...] = jnp.zeros(o_ref.shape, F32)

    xn = xn_ref[...]
    gate = _dot(xn, wg_ref[...])
    up = _dot(xn, wu_ref[...])
    act = (gate * jax.nn.sigmoid(gate) * up).astype(BF16)
    o_ref[...] += _dot(act, wd_ref[...])


def _ffn(xn, wg, wu, wd, layer, tm, th):
    m, d = xn.shape
    hid = wg.shape[2]
    return pl.pallas_call(
        _ffn_kernel,
        grid=(m // tm, hid // th),
        in_specs=[
            pl.BlockSpec((tm, d), lambda i, j: (i, 0)),
            pl.BlockSpec((None, d, th), lambda i, j: (layer, 0, j)),
            pl.BlockSpec((None, d, th), lambda i, j: (layer, 0, j)),
            pl.BlockSpec((None, th, d), lambda i, j: (layer, j, 0)),
        ],
        out_specs=pl.BlockSpec((tm, d), lambda i, j: (i, 0)),
        out_shape=jax.ShapeDtypeStruct((m, d), F32),
        compiler_params=_params("parallel", "arbitrary"),
        name="ffn",
    )(xn, wg, wu, wd)


def _ple_kernel(x_ref, y_ref, g_ref, wg_ref, p_ref, wp_ref, ng_ref, *o_refs, last):
    x = x_ref[...] + y_ref[...]
    gate = jax.nn.sigmoid(_dot(_rms(x, g_ref[...]).astype(BF16), wg_ref[...]))
    out = x + _dot(p_ref[...].astype(BF16), wp_ref[...]) * gate
    if last:
        o_refs[0][...] = _rms(out, ng_ref[...])
    else:
        o_refs[0][...] = out
        o_refs[1][...] = _rms(out, ng_ref[...]).astype(BF16)


def _ple(h, y, g, wg, p, layer, wp, next_g, tm, last):
    m, d = h.shape
    pd = p.shape[2]
    row_block = pl.BlockSpec((tm, d), lambda i: (i, 0))
    h_shape = jax.ShapeDtypeStruct((m, d), F32)

    def resident_layer(rows, cols):
        return pl.BlockSpec((None, rows, cols), lambda i: (layer, 0, 0),
                            pipeline_mode=pl.Buffered(1))

    return pl.pallas_call(
        functools.partial(_ple_kernel, last=last),
        grid=(m // tm,),
        in_specs=[
            row_block,
            row_block,
            _resident((1, d)),
            resident_layer(d, d),
            pl.BlockSpec((None, tm, pd), lambda i: (layer, i, 0)),
            resident_layer(pd, d),
            _resident((1, d)),
        ],
        out_specs=row_block if last else [row_block, row_block],
        out_shape=h_shape if last else [h_shape, jax.ShapeDtypeStruct((m, d), BF16)],
        compiler_params=_params("parallel"),
        name="ple",
    )(h, y, g.reshape(1, d), wg, p, wp, next_g.reshape(1, d))


def _key_to_float(key):
    bits = key ^ ((key >> 31) & 0x7FFFFFFF)
    return lax.bitcast_convert_type(bits, F32)


def _dsa_kernel(qi_ref, qsm_ref, ksm_ref, qa_ref, c_ref, o_ref,
                sc_ref, scb_ref, ct_ref, acc_ref, m_ref, l_ref, *, topk, qb_size, key_tile, seq):
    tq, tk = qb_size, key_tile
    qb = pl.program_id(1)
    nk = (qb + 1) * (tq // tk)

    @pl.when(qb == 0)
    def _():
        ct_ref[...] = jnp.transpose(c_ref[...].astype(F32)).astype(BF16)

    row = lax.broadcasted_iota(I32, (tk, tq), 0)
    col = lax.broadcasted_iota(I32, (tk, tq), 1)
    limit = (((qb * tq + col) >> CHUNK_SHIFT) + 1) << CHUNK_SHIFT

    w_t = jnp.transpose(qsm_ref[...].astype(F32))[SM_W:SM_W + IDX_HEADS, :]

    def score_tile(kt, carry):
        rows = pl.ds(pl.multiple_of(kt * tk, tk), tk)
        kk = ksm_ref[rows, :][:, :IDX_DIM]
        sc = jnp.zeros((tk, tq), F32)
        for h in range(IDX_HEADS):
            qh = qi_ref[:, h * IDX_DIM:(h + 1) * IDX_DIM]
            sc = sc + jnp.maximum(_dot_nt(kk, qh), 0.0) * w_t[h:h + 1, :]
        sc = sc * (IDX_DIM ** -0.5 * IDX_HEADS ** -0.5)
        sc = jnp.where(kt * tk + row < limit, sc, -jnp.inf)
        sc_ref[rows, :] = sc
        scb_ref[rows, :] = sc.astype(BF16)
        return carry

    lax.fori_loop(0, nk, score_tile, 0)

    def count(pred):
        def body(kt, acc):
            rows = pl.ds(pl.multiple_of(kt * tk, tk), tk)
            hit = pred(sc_ref[rows, :]).astype(I32)
            return acc + jnp.sum(hit.reshape(tk // SUBLANES, SUBLANES, tq), axis=0)
        acc = lax.fori_loop(0, nk, body, jnp.zeros((SUBLANES, tq), I32))
        return jnp.sum(acc, axis=0, keepdims=True)

    def count_ge(key):
        cand = _key_to_float(key)
        return count(lambda s: s >= cand)

    pack = 2 * SUBLANES
    low_mask = (1 << BF16_DROPPED_BITS) - 1

    def count_ge_coarse(key):
        bits = (key ^ ((key >> 31) & 0x7FFFFFFF)) & ~low_mask
        cand = lax.bitcast_convert_type(bits, F32).astype(BF16)

        def body(kt, acc):
            rows = pl.ds(pl.multiple_of(kt * tk, tk), tk)
            hit = jnp.where(scb_ref[rows, :] >= cand, jnp.ones((), BF16), jnp.zeros((), BF16))
            part = hit[0:pack, :]
            for r in range(1, tk // pack):
                part = part + hit[r * pack:(r + 1) * pack, :]
            return acc + part.astype(F32)

        acc = lax.fori_loop(0, nk, body, jnp.zeros((pack, tq), F32))
        return jnp.sum(acc, axis=0, keepdims=True)

    zero_key = jnp.zeros((1, tq), I32)
    res0 = jnp.where(count_ge_coarse(zero_key) >= topk, 0, INT_MIN).astype(I32)

    def bisect_coarse(i, res):
        cand = res | jnp.left_shift(jnp.int32(1), 30 - i)
        return jnp.where(count_ge_coarse(cand) >= topk, cand, res)

    coarse = lax.fori_loop(0, 31 - BF16_DROPPED_BITS, bisect_coarse, res0)
    coarse = jnp.maximum(coarse, KEY_NEG_INF - low_mask)
    center = jnp.where(coarse < 0, coarse + low_mask, coarse)
    half_step = 1 << (BF16_DROPPED_BITS - 1)
    lo = center - (half_step + 1)

    def bisect_fine(i, res):
        cand = res + jnp.left_shift(jnp.int32(1), BF16_DROPPED_BITS - i)
        return jnp.where(count_ge(cand) >= topk, cand, res)

    res = lax.fori_loop(0, BF16_DROPPED_BITS + 1, bisect_fine, lo)
    thr = _key_to_float(jnp.maximum(res, KEY_NEG_INF))
    n_ge = count(lambda s: s >= thr) + jnp.where(thr == -jnp.inf, seq - nk * tk, 0)

    @pl.when(jnp.max(n_ge) > topk)
    def _():
        need = (topk - count(lambda s: s > thr)).astype(F32)
        tri = (lax.broadcasted_iota(I32, (tk, tk), 0)
               >= lax.broadcasted_iota(I32, (tk, tk), 1)).astype(BF16)

        def body(kt, seen):
            rows = pl.ds(pl.multiple_of(kt * tk, tk), tk)
            s = sc_ref[rows, :]
            eq = s == thr
            rank = _dot(tri, eq.astype(BF16)) + seen
            drop = (eq & (rank > need)) | (kt * tk + row >= limit)
            sc_ref[rows, :] = jnp.where(drop, jnp.nan, s)
            return seen + jnp.sum(eq.astype(F32), axis=0, keepdims=True)

        lax.fori_loop(0, nk, body, jnp.zeros((1, tq), F32))

    m_ref[...] = jnp.full(m_ref.shape, -1e30, F32)
    l_ref[...] = jnp.zeros(l_ref.shape, F32)
    acc_ref[...] = jnp.zeros(acc_ref.shape, F32)

    heads = range(A_HEADS)
    hsl = [slice(h * A_LATENT, (h + 1) * A_LATENT) for h in heads]
    c2 = (A_LATENT ** -0.5) * LOG2_E

    def attend(kt, carry):
        rows = pl.ds(pl.multiple_of(kt * tk, tk), tk)
        valid = sc_ref[rows, :] >= thr
        ck = c_ref[rows, :]
        ckt = ct_ref[:, rows]
        raws = [jnp.where(valid, _dot_nt(ck, qa_ref[:, hs]), -1e30) for hs in hsl]
        ps, alphas = [], []
        for h, raw in zip(heads, raws):
            m_old = m_ref[h:h + 1, :]
            m_new = jnp.maximum(m_old, jnp.max(raw, axis=0, keepdims=True))
            alpha = jnp.exp2((m_old - m_new) * c2)
            p = jnp.exp2((raw - m_new) * c2)
            l_ref[h:h + 1, :] = alpha * l_ref[h:h + 1, :] + jnp.sum(p, axis=0, keepdims=True)
            m_ref[h:h + 1, :] = m_new
            ps.append(p.astype(BF16))
            alphas.append(alpha)
        pvs = [_dot(ckt, p) for p in ps]
        for hs, alpha, pv in zip(hsl, alphas, pvs):
            acc_ref[hs, :] = alpha * acc_ref[hs, :] + pv
        return carry

    lax.fori_loop(0, nk, attend, 0)

    for h in heads:
        out = acc_ref[hsl[h], :] / l_ref[h:h + 1, :]
        o_ref[:, hsl[h]] = jnp.transpose(out).astype(o_ref.dtype)


def _dsa(u, batch, seq, qb_size, key_tile):
    m = batch * seq
    nqb = seq // qb_size
    assert qb_size % key_tile == 0 and key_tile % CHUNK == 0
    topk = min(A_TOPK_MAX, seq // 4)
    wide = A_HEADS * A_LATENT
    return pl.pallas_call(
        functools.partial(_dsa_kernel, topk=topk, qb_size=qb_size, key_tile=key_tile, seq=seq),
        grid=(batch, nqb),
        in_specs=[
            pl.BlockSpec((qb_size, wide), lambda b, q: (b * nqb + q, U_QI // wide)),
            pl.BlockSpec((qb_size, LANES), lambda b, q: (b * nqb + q, U_SMALL // LANES)),
            pl.BlockSpec((seq, LANES), lambda b, q: (b, U_SMALL // LANES)),
            pl.BlockSpec((qb_size, wide), lambda b, q: (b * nqb + q, U_QA // wide)),
            pl.BlockSpec((seq, LANES), lambda b, q: (b, U_CA // LANES)),
        ],
        out_specs=pl.BlockSpec((qb_size, wide), lambda b, q: (b * nqb + q, 0)),
        out_shape=jax.ShapeDtypeStruct((m, wide), BF16),
        scratch_shapes=[
            pltpu.VMEM((seq, qb_size), F32),
            pltpu.VMEM((seq, qb_size), BF16),
            pltpu.VMEM((A_LATENT, seq), BF16),
            pltpu.VMEM((wide, qb_size), F32),
            pltpu.VMEM((A_HEADS, qb_size), F32),
            pltpu.VMEM((A_HEADS, qb_size), F32),
        ],
        compiler_params=_params("parallel", "arbitrary"),
        name="dsa",
    )(u, u, u, u, u)


def _gla_kernel(q_ref, k_ref, v_ref, g_ref, sm_ref, wgk_ref, bgk_ref, gn_ref, o_ref, st_ref,
                *, nchunks):
    c = CHUNK
    n = GLA_CHUNKS_PER_STEP
    span = n * c
    row = lax.broadcasted_iota(I32, (span, span), 0)
    col = lax.broadcasted_iota(I32, (span, span), 1)
    tri = ((row >= col) & ((row >> CHUNK_SHIFT) == (col >> CHUNK_SHIFT))).astype(BF16)
    causal = lax.broadcasted_iota(I32, (c, c), 0) >= lax.broadcasted_iota(I32, (c, c), 1)
    st_ref[...] = jnp.zeros(st_ref.shape, F32)
    wgk = wgk_ref[...]
    bgk = bgk_ref[...]
    gn = gn_ref[...]
    heads = range(B_HEADS)
    ksl = [slice(h * B_DK, (h + 1) * B_DK) for h in heads]
    vsl = [slice(h * B_DV, (h + 1) * B_DV) for h in heads]
    work = [(s, h) for s in range(n) for h in heads]

    def step(si, carry):
        base = pl.multiple_of(si * span, span)
        rows = pl.ds(base, span)
        x = _dot(sm_ref[rows, :], wgk) + bgk
        gk = -(jnp.maximum(-x, 0.0) + _softplus_tail(x)) / B_GATE_NORM
        hi, lo = _split_bf16(gk)
        cum_all = _dot(tri, hi) + _dot(tri, lo)
        q_es, k_es, k_ds, decays, vs_ = {}, {}, {}, {}, {}
        for s, h in work:
            sub = pl.ds(base + s * c, c)
            cum = cum_all[s * c:(s + 1) * c, ksl[h]]
            last = cum[c - 1:c, :]
            q = q_ref[sub, ksl[h]].astype(F32) * (B_DK ** -0.5)
            k = k_ref[sub, ksl[h]].astype(F32)
            q_es[s, h] = (q * jnp.exp(cum)).astype(BF16)
            k_es[s, h] = (k * jnp.exp(-cum)).astype(BF16)
            k_ds[s, h] = (k * jnp.exp(last - cum)).astype(BF16)
            decays[s, h] = jnp.exp(last)
            vs_[s, h] = v_ref[sub, vsl[h]]
        atts = {w: _dot_nt(q_es[w], k_es[w]) for w in work}
        kvs = {w: lax.dot_general(vs_[w], k_ds[w], TN_DIMS, preferred_element_type=F32)
               for w in work}
        sts = {h: st_ref[h] for h in heads}
        inter = {}
        for s in range(n):
            for h in heads:
                inter[s, h] = _dot_nt(q_es[s, h], sts[h].astype(BF16))
            for h in heads:
                sts[h] = sts[h] * decays[s, h] + kvs[s, h]
        for h in heads:
            st_ref[h] = sts[h]
        atts = {w: jnp.where(causal, atts[w], 0.0).astype(BF16) for w in work}
        intra = {w: _dot(atts[w], vs_[w]) for w in work}
        for s, h in work:
            sub = pl.ds(base + s * c, c)
            gate = g_ref[sub, vsl[h]].astype(F32)
            o = intra[s, h] + inter[s, h]
            o_ref[sub, vsl[h]] = (_rms(o, gn) * (gate * jax.nn.sigmoid(gate))).astype(o_ref.dtype)
        return carry

    lax.fori_loop(0, nchunks // n, step, 0)


def _gla(u, wgk_pad, bgk, gnorm, batch, seq):
    m = batch * seq
    qk_wide = B_HEADS * B_DK
    v_wide = B_HEADS * B_DV
    return pl.pallas_call(
        functools.partial(_gla_kernel, nchunks=seq // CHUNK),
        grid=(batch,),
        in_specs=[
            pl.BlockSpec((seq, qk_wide), lambda b: (b, U_QB // qk_wide)),
            pl.BlockSpec((seq, qk_wide), lambda b: (b, U_KB // qk_wide)),
            pl.BlockSpec((seq, v_wide), lambda b: (b, U_VB // v_wide)),
            pl.BlockSpec((seq, v_wide), lambda b: (b, U_GB // v_wide)),
            pl.BlockSpec((seq, LANES), lambda b: (b, U_SMALL // LANES)),
            pl.BlockSpec((LANES, qk_wide), lambda b: (0, 0)),
            pl.BlockSpec((1, qk_wide), lambda b: (0, 0)),
            pl.BlockSpec((1, B_DV), lambda b: (0, 0)),
        ],
        out_specs=pl.BlockSpec((seq, v_wide), lambda b: (b, 0)),
        out_shape=jax.ShapeDtypeStruct((m, v_wide), BF16),
        scratch_shapes=[pltpu.VMEM((B_HEADS, B_DV, B_DK), F32)],
        compiler_params=_params("parallel"),
        name="gla",
    )(u, u, u, u, u, wgk_pad, bgk.reshape(1, -1), gnorm.reshape(1, -1))


def _sb_kernel(q_ref, k_ref, v_ref, o_ref, *, blk, nblk, head_dim, heads):
    t = blk
    d = head_dim
    row = lax.broadcasted_iota(I32, (t, t), 0)
    col = lax.broadcasted_iota(I32, (t, t), 1)
    upper = (row > col).astype(BF16)
    diag = col < row
    scale = d ** -0.5

    hsl = [slice(h * d, (h + 1) * d) for h in range(heads)]

    def block(qrows, krows, runs, mask):
        zs = [_dot_nt(q_ref[qrows, hs], k_ref[krows, hs]) * scale for hs in hsl]
        lms, log_bs = [], []
        for z in zs:
            log_1mb = -(jnp.maximum(z, 0.0) + jnp.log(1.0 + jnp.exp(-jnp.abs(z))))
            log_bs.append(log_1mb + z)
            lms.append(log_1mb if mask is None else jnp.where(mask, log_1mb, 0.0))
        sufs = [_dot(lm.astype(BF16), upper) for lm in lms]
        ws = []
        for suf, log_b, run in zip(sufs, log_bs, runs):
            w = jnp.exp(suf + run + log_b)
            ws.append((w if mask is None else jnp.where(mask, w, 0.0)).astype(BF16))
        outs = [_dot(w, v_ref[krows, hs]) for w, hs in zip(ws, hsl)]
        new_runs = [run + jnp.sum(lm, axis=1, keepdims=True) for run, lm in zip(runs, lms)]
        return new_runs, outs

    def q_block(qb, carry):
        qrows = pl.ds(pl.multiple_of(qb * t, t), t)
        runs, accs = block(qrows, qrows, [jnp.zeros((t, 1), F32)] * heads, diag)

        def alive(rs):
            top = jnp.max(rs[0])
            for r in rs[1:]:
                top = jnp.maximum(top, jnp.max(r))
            return (top >= SB_EXP_UNDERFLOW).astype(I32)

        def cond(state):
            i, live, _, _ = state
            return jnp.logical_and(i <= qb, live > 0)

        def k_block(state):
            i, _, rs, acs = state
            krows = pl.ds(pl.multiple_of((qb - i) * t, t), t)
            new_rs, outs = block(qrows, krows, list(rs), None)
            new_acs = [a + o for a, o in zip(acs, outs)]
            return i + 1, alive(new_rs), tuple(new_rs), tuple(new_acs)

        state = (jnp.int32(1), alive(runs), tuple(runs), tuple(accs))
        _, _, _, accs = lax.while_loop(cond, k_block, state)
        for h in range(heads):
            o_ref[qrows, h * d:(h + 1) * d] = accs[h].astype(o_ref.dtype)
        return carry

    lax.fori_loop(0, nblk, q_block, 0)


def _stick_breaking(qkv, batch, seq, blk):
    m = batch * seq
    d = qkv.shape[1] // (3 * C_HEADS)
    groups = C_HEADS // SB_HEADS_PER_STEP
    wide = SB_HEADS_PER_STEP * d
    return pl.pallas_call(
        functools.partial(_sb_kernel, blk=blk, nblk=seq // blk, head_dim=d, heads=SB_HEADS_PER_STEP),
        grid=(batch, groups),
        in_specs=[
            pl.BlockSpec((seq, wide), lambda b, g: (b, g)),
            pl.BlockSpec((seq, wide), lambda b, g: (b, groups + g)),
            pl.BlockSpec((seq, wide), lambda b, g: (b, 2 * groups + g)),
        ],
        out_specs=pl.BlockSpec((seq, wide), lambda b, g: (b, g)),
        out_shape=jax.ShapeDtypeStruct((m, C_HEADS * d), BF16),
        compiler_params=_params("parallel", "arbitrary"),
        name="stick_breaking",
    )(qkv, qkv, qkv)


_EVEN_WIDTHS = (A_HEADS * A_LATENT, A_LATENT, IDX_HEADS * IDX_DIM, IDX_DIM, IDX_HEADS,
                B_HEADS * B_DK, B_HEADS * B_DK, B_HEADS * B_DV, B_HEADS * B_DV, B_GATE_RANK)
_EVEN_STARTS = [sum(_EVEN_WIDTHS[:i]) for i in range(len(_EVEN_WIDTHS))]
_U_ORDER = (0, 2, 7, 8, 5, 6, 1, 3, 4, 9)
EVEN_SRC_COLS = tuple((_EVEN_STARTS[i], _EVEN_STARTS[i] + _EVEN_WIDTHS[i]) for i in _U_ORDER)


def _permute_w_in_kernel(w_ref, o_ref):
    w = w_ref[...]
    parts = [w[:, a:b] for a, b in EVEN_SRC_COLS]
    used = sum(b - a for a, b in EVEN_SRC_COLS)
    parts.append(jnp.zeros((w.shape[0], U_WIDTH - used), w.dtype))
    o_ref[...] = jnp.concatenate(parts, axis=1).astype(o_ref.dtype)


def _permute_even_w_in(w, rows):
    d, n = w.shape
    return pl.pallas_call(
        _permute_w_in_kernel,
        grid=(d // rows,),
        in_specs=[pl.BlockSpec((rows, n), lambda i: (i, 0))],
        out_specs=pl.BlockSpec((rows, U_WIDTH), lambda i: (i, 0)),
        out_shape=jax.ShapeDtypeStruct((d, U_WIDTH), BF16),
        compiler_params=_params("parallel"),
        name="permute_w_in",
    )(w)


def kernel(x, p, even_norm, even_w_in, even_w_gk, even_b_gk, even_gla_norm, even_w_out,
           odd_norm, odd_w_in, odd_w_out, ffn_norm, ffn_w_gate, ffn_w_up, ffn_w_down,
           ple_norm, ple_w_gate, ple_w_proj, final_norm):
    batch, seq, d = x.shape
    depth = p.shape[0]
    m = batch * seq
    h = x.reshape(m, d)
    tm = _tile(m, 1024)
    tm_row = _tile(m, 512)
    tn_pref = 1792
    blk = _tile(seq, 256)
    dsa_blk = _tile(seq, DSA_BLOCK)

    def mixer_norm(i):
        return even_norm[i // 2] if i % 2 == 0 else odd_norm[i // 2]

    def in_proj(h, hn, g, w, tn):
        return _norm_matmul(h, g, w, tm, tn) if hn is None else _matmul(hn, w, tm, tn)

    ffn_wg, ffn_wu, ffn_wd = (w.astype(BF16) for w in (ffn_w_gate, ffn_w_up, ffn_w_down))
    ple_wg, ple_wp = ple_w_gate.astype(BF16), ple_w_proj.astype(BF16)

    hn = None
    for i in range(depth):
        j = i // 2
        if i % 2 == 0:
            w_in = _permute_even_w_in(even_w_in[j], _tile(d, 256))
            u = in_proj(h, hn, even_norm[j], w_in, _tile(U_WIDTH, tn_pref))
            o_a = _dsa(u, batch, seq, dsa_blk, dsa_blk)
            wgk_pad = jnp.zeros((LANES, B_HEADS * B_DK), F32)
            wgk_pad = wgk_pad.at[SM_GK:SM_GK + B_GATE_RANK].set(even_w_gk[j]).astype(BF16)
            o_b = _gla(u, wgk_pad, even_b_gk[j], even_gla_norm[j], batch, seq)
            h, xn = _proj_residual(h, ffn_norm[i], [o_a, o_b], even_w_out[j].astype(BF16), tm_row)
        else:
            qkv = in_proj(h, hn, odd_norm[j], odd_w_in[j].astype(BF16), _tile(3 * d, tn_pref))
            o = _stick_breaking(qkv, batch, seq, blk)
            h, xn = _proj_residual(h, ffn_norm[i], [o], odd_w_out[j].astype(BF16), tm_row)
        y = _ffn(xn, ffn_wg, ffn_wu, ffn_wd, i, tm, _tile(ffn_wg.shape[2], 512))
        last = i == depth - 1
        out = _ple(h, y, ple_norm[i], ple_wg, p.reshape(depth, m, -1), i, ple_wp,
                   final_norm if last else mixer_norm(i + 1), tm_row, last)
        h, hn = (out, None) if last else out
    return h.reshape(batch, seq, d)
```
